```python
import jax, jax.numpy as jnp
from jax import lax
import numpy as np

D_MODEL = 2048
BATCH = 4
SEQ = 4096
DEPTH = 1

HG_HEADS = 8
HG_DK = 128
HG_DV = 128
HG_K = HG_HEADS * HG_DK
HG_V = HG_HEADS * HG_DV
HGRN_CHUNK = 64
GM_GROUPS = 8
GM_CH = 128
GM_WIDTH = GM_GROUPS * GM_CH
GM_CHUNK = 128
MIX_WIDTH = HG_V + GM_WIDTH
IN_SPLITS = (HG_K, 2 * HG_K, 2 * HG_K + HG_V, 2 * HG_K + 2 * HG_V, 2 * HG_K + 2 * HG_V + GM_WIDTH)
IN_WIDTH = 2 * HG_K + 2 * HG_V + 2 * GM_WIDTH
N_EXPERTS = 32
TOP_K = 4
D_FF = 2048
SWIGLU_LIMIT = 7.0
SWIGLU_ALPHA = 1.702
MOE_BLOCK = 256
PLE_DIM = 256
RMS_EPS = 1e-6
LN_EPS = 1e-5

kernel_name = "hybrid_hgrn2_gmlp_moe_block"


def rms_norm(x, g):
    xf = x.astype(jnp.float32)
    y = xf * lax.rsqrt(jnp.mean(xf * xf, axis=-1, keepdims=True) + RMS_EPS)
    return (y * g.astype(jnp.float32)).astype(x.dtype)


def layer_norm(x, g, b):
    xf = x.astype(jnp.float32)
    mu = jnp.mean(xf, axis=-1, keepdims=True)
    xc = xf - mu
    y = xc * lax.rsqrt(jnp.mean(xc * xc, axis=-1, keepdims=True) + LN_EPS)
    return (y * g.astype(jnp.float32) + b.astype(jnp.float32)).astype(x.dtype)


def hgrn2_recurrence(q, f_logit, v, lb):
    B, S = q.shape[0], q.shape[1]
    n = S // HGRN_CHUNK
    q = jax.nn.silu(q.astype(jnp.float32))
    f = lb + (1.0 - lb) * jax.nn.sigmoid(f_logit.astype(jnp.float32))
    log_f = jnp.log(f)
    k = 1.0 - f

    def to_chunks(t):
        return t.reshape(B, n, HGRN_CHUNK, HG_HEADS, t.shape[-1]).transpose(1, 0, 3, 2, 4)

    causal = jnp.tril(jnp.ones((HGRN_CHUNK, HGRN_CHUNK), dtype=bool))

    def step(state, inp):
        qc, kc, vc, gc = inp
        b = jnp.cumsum(gc, axis=2)
        o_inter = jnp.einsum('bhtd,bhde->bhte', qc * jnp.exp(b), state)
        diff = b[:, :, :, None, :] - b[:, :, None, :, :]
        decay = jnp.exp(jnp.where(causal[:, :, None], diff, -jnp.inf))
        scores = jnp.einsum('bhtd,bhsd,bhtsd->bhts', qc, kc, decay)
        o = o_inter + jnp.einsum('bhts,bhse->bhte', scores, vc)
        b_last = b[:, :, -1:, :]
        new_state = jnp.exp(b_last[:, :, 0, :])[..., None] * state + jnp.einsum(
            'bhsd,bhse->bhde', kc * jnp.exp(b_last - b), vc)
        return new_state, o

    s0 = jnp.zeros((B, HG_HEADS, HG_DK, HG_DV), jnp.float32)
    _, o = lax.scan(step, s0, (to_chunks(q), to_chunks(k), to_chunks(v.astype(jnp.float32)), to_chunks(log_f)))
    return o.transpose(1, 0, 3, 2, 4).reshape(B, S, HG_HEADS, HG_DV)


def chunked_spatial_gating(u, v, ln_g, ln_b, w_s, b_s):
    B, S = u.shape[0], u.shape[1]
    n = S // GM_CHUNK
    u = jax.nn.gelu(u, approximate=False)
    v = layer_norm(jax.nn.gelu(v, approximate=False), ln_g, ln_b)
    vb = v.reshape(B, n, GM_CHUNK, GM_GROUPS, GM_CH)
    w = w_s * jnp.tril(jnp.ones((GM_CHUNK, GM_CHUNK), w_s.dtype))
    mixed = jnp.einsum('gts,bnsgc->bntgc', w, vb) + b_s.T[:, :, None]
    return u * mixed.reshape(B, S, GM_WIDTH)


def moe_ffn(h, w_router, b_router, w_gu, b_gu, w_dn, b_dn):
    T, D = h.shape
    logits = (h @ w_router + b_router).astype(jnp.float32)
    top_val, top_idx = lax.top_k(logits, TOP_K)
    gates = jax.nn.softmax(top_val, axis=-1).astype(h.dtype)
    A = T * TOP_K
    e_flat = top_idx.reshape(-1).astype(jnp.int32)
    tok_flat = jnp.arange(A, dtype=jnp.int32) // TOP_K
    g_flat = gates.reshape(-1)
    order = jnp.argsort(e_flat)
    e_sorted, tok_sorted, g_sorted = e_flat[order], tok_flat[order], g_flat[order]
    counts = jnp.zeros((N_EXPERTS,), jnp.int32).at[e_flat].add(1)
    starts = jnp.cumsum(counts) - counts
    padded = (counts + MOE_BLOCK - 1) // MOE_BLOCK * MOE_BLOCK
    pad_ends = jnp.cumsum(padded)
    pad_starts = pad_ends - padded
    dest = pad_starts[e_sorted] + (jnp.arange(A, dtype=jnp.int32) - starts[e_sorted])
    n_blocks = -(-A // MOE_BLOCK) + N_EXPERTS
    n_rows = n_blocks * MOE_BLOCK
    row_tok = jnp.zeros((n_rows,), jnp.int32).at[dest].set(tok_sorted)
    row_gate = jnp.zeros((n_rows,), h.dtype).at[dest].set(g_sorted)
    block_start = jnp.arange(n_blocks, dtype=jnp.int32) * MOE_BLOCK
    block_expert = jnp.minimum(jnp.searchsorted(pad_ends, block_start, side='right'), N_EXPERTS - 1)

    def expert_block(args):
        toks, gts, e = args
        xb = h[toks]
        gu = xb @ w_gu[e] + b_gu[e]
        gate = jnp.minimum(gu[:, 0::2], SWIGLU_LIMIT)
        up = jnp.clip(gu[:, 1::2], -SWIGLU_LIMIT, SWIGLU_LIMIT)
        act = (up + 1.0) * gate * jax.nn.sigmoid(SWIGLU_ALPHA * gate)
        y = act @ w_dn[e] + b_dn[e]
        return y * gts[:, None]

    ys = lax.map(expert_block, (row_tok.reshape(n_blocks, MOE_BLOCK),
                                row_gate.reshape(n_blocks, MOE_BLOCK), block_expert))
    return jax.ops.segment_sum(ys.reshape(n_rows, D), row_tok, num_segments=T)


def setup_inputs(seed: int = 0) -> dict:
    key = jax.random.key(seed)
    ks = jax.random.split(key, 24)
    f32 = jnp.float32
    nrm = lambda k, s: jax.random.normal(k, s, f32)
    L = DEPTH
    return {
        "x": nrm(ks[0], (BATCH, SEQ, D_MODEL)),
        "p": nrm(ks[1], (DEPTH, BATCH, SEQ, PLE_DIM)),
        "norm_mix": 1.0 + 0.02 * nrm(ks[2], (L, D_MODEL)),
        "w_in": nrm(ks[3], (L, D_MODEL, IN_WIDTH)) * D_MODEL ** -0.5,
        "lb_logits": 0.5 * nrm(ks[4], (DEPTH + 1, HG_K)),
        "hgrn_out_norm": 1.0 + 0.02 * nrm(ks[5], (L, HG_V)),
        "gmlp_ln_g": 1.0 + 0.02 * nrm(ks[6], (L, GM_WIDTH)),
        "gmlp_ln_b": 0.02 * nrm(ks[7], (L, GM_WIDTH)),
        "w_spatial": nrm(ks[8], (L, GM_GROUPS, GM_CHUNK, GM_CHUNK)) * GM_CHUNK ** -0.5,
        "b_spatial": 1.0 + 0.02 * nrm(ks[9], (L, GM_GROUPS, GM_CHUNK)),
        "w_out": nrm(ks[10], (L, MIX_WIDTH, D_MODEL)) * MIX_WIDTH ** -0.5,
        "norm_ffn": 1.0 + 0.02 * nrm(ks[11], (L, D_MODEL)),
        "w_router": nrm(ks[12], (L, D_MODEL, N_EXPERTS)) * D_MODEL ** -0.5,
        "b_router": 0.01 * nrm(ks[13], (L, N_EXPERTS)),
        "w_gate_up": nrm(ks[14], (L, N_EXPERTS, D_MODEL, 2 * D_FF)) * D_MODEL ** -0.5,
        "b_gate_up": 0.02 * nrm(ks[15], (L, N_EXPERTS, 2 * D_FF)),
        "w_down": nrm(ks[16], (L, N_EXPERTS, D_FF, D_MODEL)) * D_FF ** -0.5,
        "b_down": 0.02 * nrm(ks[17], (L, N_EXPERTS, D_MODEL)),
        "norm_ple": 1.0 + 0.02 * nrm(ks[18], (L, D_MODEL)),
        "w_ple_gate": nrm(ks[19], (L, D_MODEL, D_MODEL)) * D_MODEL ** -0.5,
        "w_ple_proj": nrm(ks[20], (L, PLE_DIM, D_MODEL)) * PLE_DIM ** -0.5,
        "norm_final": 1.0 + 0.02 * nrm(ks[21], (D_MODEL,)),
    }


def reference(x, p, norm_mix, w_in, lb_logits, hgrn_out_norm, gmlp_ln_g, gmlp_ln_b, w_spatial,
              b_spatial, w_out, norm_ffn, w_router, b_router, w_gate_up, b_gate_up, w_down, b_down,
              norm_ple, w_ple_gate, w_ple_proj, norm_final):
    B, S, D = x.shape
    lower_bounds = jnp.cumsum(jax.nn.softmax(lb_logits.astype(jnp.float32), axis=0), axis=0)
    for i in range(DEPTH):
        h = rms_norm(x, norm_mix[i])
        z = h @ w_in[i]
        q, f_logit, v_in, o_gate, u, v = jnp.split(z, IN_SPLITS, axis=-1)
        lb = lower_bounds[i].reshape(HG_HEADS, HG_DK)
        o = hgrn2_recurrence(q.reshape(B, S, HG_HEADS, HG_DK), f_logit.reshape(B, S, HG_HEADS, HG_DK),
                             v_in.reshape(B, S, HG_HEADS, HG_DV), lb)
        o = rms_norm(o, hgrn_out_norm[i].reshape(HG_HEADS, HG_DV)).reshape(B, S, HG_V).astype(x.dtype)
        o = o * jax.nn.sigmoid(o_gate)
        sg = chunked_spatial_gating(u, v, gmlp_ln_g[i], gmlp_ln_b[i], w_spatial[i], b_spatial[i])
        x = x + jnp.concatenate([o, sg], axis=-1) @ w_out[i]
        h = rms_norm(x, norm_ffn[i])
        x = x + moe_ffn(h.reshape(B * S, D), w_router[i], b_router[i], w_gate_up[i], b_gate_up[i],
                        w_down[i], b_down[i]).reshape(B, S, D)
        gate = jax.nn.sigmoid(rms_norm(x, norm_ple[i]) @ w_ple_gate[i])
        x = x + gate * (p[i] @ w_ple_proj[i])
    return rms_norm(x, norm_final)
```

```python
import functools
import math

import numpy as np
import jax
import jax.numpy as jnp
from jax import lax
from jax.experimental import pallas as pl
from jax.experimental.pallas import tpu as pltpu

F32 = jnp.float32
BF16 = jnp.bfloat16

HG_HEADS = 8
HG_D = 128
GM_GROUPS = 8
GM_CH = 128
GM_CHUNK = 128
N_EXPERTS = 32
TOP_K = 4
SWIGLU_LIMIT = 7.0
SWIGLU_ALPHA = 1.702
RMS_EPS = 1e-6
LN_EPS = 1e-5

LANES = 128
HGRN_TILE = 256
EXPERT_BLOCK = 256
VMEM_LIMIT = 56 * 1024 * 1024


def _cparams(n_axes):
    return pltpu.CompilerParams(dimension_semantics=("arbitrary",) * n_axes,
                                vmem_limit_bytes=VMEM_LIMIT)


def _rms(x, g, eps):
    return x * lax.rsqrt(jnp.mean(x * x, axis=-1, keepdims=True) + eps) * g


def _dot(a, b):
    return jnp.dot(a, b, preferred_element_type=F32)


def _dot_nt(a, b):
    return lax.dot_general(a, b, (((1,), (1,)), ((), ())), preferred_element_type=F32)


def _gelu(x):
    return 0.5 * x * (1.0 + lax.erf(x * (1.0 / math.sqrt(2.0))))


def _inproj_body(x_ref, g_ref, w_ref, z_ref, h_scr):
    @pl.when(pl.program_id(1) == 0)
    def _():
        h_scr[...] = _rms(x_ref[...], g_ref[...], RMS_EPS).astype(BF16)

    z_ref[...] = _dot(h_scr[...], w_ref[...])


def _inproj(x2d, g, w, tm=1024, tn=1024):
    T, D = x2d.shape
    N = w.shape[1]
    return pl.pallas_call(
        _inproj_body,
        grid=(T // tm, N // tn),
        in_specs=[pl.BlockSpec((tm, D), lambda i, j: (i, 0)),
                  pl.BlockSpec((1, D), lambda i, j: (0, 0)),
                  pl.BlockSpec((D, tn), lambda i, j: (0, j))],
        out_specs=pl.BlockSpec((tm, tn), lambda i, j: (i, j)),
        out_shape=jax.ShapeDtypeStruct((T, N), F32),
        scratch_shapes=[pltpu.VMEM((tm, D), BF16)],
        compiler_params=_cparams(2),
        name="inproj",
    )(x2d, g.reshape(1, D), w)


def _hgrn_constants(C):
    nlev = int(math.log2(C))
    r = np.arange(C)
    mats = [(r[None, :] <= r[:, None]), (r[None, :] > r[:, None])]
    level = np.full((C, C), -1, np.int32)
    level[r, r] = 0
    for l in range(1, nlev + 1):
        L = C >> l
        pos = r % (2 * L)
        mid = (r // (2 * L)) * (2 * L) + L - 1
        second = pos >= L
        m = np.where(second[:, None],
                     (r[None, :] > mid[:, None]) & (r[None, :] <= r[:, None]),
                     (r[None, :] > r[:, None]) & (r[None, :] <= mid[:, None]))
        mats.append(m)
        same = (r[:, None] // (2 * L)) == (r[None, :] // (2 * L))
        level[same & second[:, None] & (~second)[None, :]] = l
    w = np.concatenate(mats, axis=0).astype(np.float32)
    return jnp.asarray(w, BF16), jnp.asarray(level), nlev


def _hgrn_body(q_ref, f_ref, v_ref, og_ref, lb_ref, gn_ref, w_ref, lv_ref, o_ref, st_scr, *, nlev):
    C = q_ref.shape[0]

    @pl.when(pl.program_id(2) == 0)
    def _():
        st_scr[...] = jnp.zeros_like(st_scr)

    q = q_ref[...]
    q = q * jax.nn.sigmoid(q)
    lb = lb_ref[...]
    f = lb + (1.0 - lb) * jax.nn.sigmoid(f_ref[...])
    g = jnp.log(f)
    k = 1.0 - f
    v = v_ref[...]

    g1 = g.astype(BF16)
    r1 = g - g1.astype(F32)
    g2 = r1.astype(BF16)
    g3 = (r1 - g2.astype(F32)).astype(BF16)
    gcat = jnp.concatenate([g1, g2, g3], axis=1)

    def range_sum(i):
        e3 = _dot(w_ref[i * C:(i + 1) * C, :], gcat)
        return e3[:, :HG_D] + e3[:, HG_D:2 * HG_D] + e3[:, 2 * HG_D:]

    b = range_sum(0)
    suffix = range_sum(1)
    lv = lv_ref[...]

    a = jnp.where(lv == 0, _dot_nt(q.astype(BF16), k.astype(BF16)), 0.0)
    for l in range(1, nlev + 1):
        e = jnp.exp(range_sum(1 + l))
        a = jnp.where(lv == l, _dot_nt((q * e).astype(BF16), (k * e).astype(BF16)), a)

    st = st_scr[...]
    o = _dot_nt((q * jnp.exp(b)).astype(BF16), st.astype(BF16))
    o = o + _dot(a.astype(BF16), v.astype(BF16))

    khat = (k * jnp.exp(suffix)).astype(BF16)
    st_scr[...] = st * jnp.exp(b[C - 1:C, :]) + _dot(v.T.astype(BF16), khat)

    o = _rms(o, gn_ref[...], RMS_EPS)
    o_ref[...] = (o * jax.nn.sigmoid(og_ref[...])).astype(BF16)


def _hgrn(z, lb, gn, B, S):
    C = HGRN_TILE
    H = HG_HEADS
    n_s = S // C
    w, lv, nlev = _hgrn_constants(C)
    row = lambda b, h, s: b * n_s + s
    col = lambda off: pl.BlockSpec((C, HG_D), lambda b, h, s: (row(b, h, s), off * H + h))
    per_head = pl.BlockSpec((None, 1, HG_D), lambda b, h, s: (h, 0, 0))
    return pl.pallas_call(
        functools.partial(_hgrn_body, nlev=nlev),
        grid=(B, H, n_s),
        in_specs=[col(0), col(1), col(2), col(3), per_head, per_head,
                  pl.BlockSpec(w.shape, lambda b, h, s: (0, 0)),
                  pl.BlockSpec(lv.shape, lambda b, h, s: (0, 0))],
        out_specs=pl.BlockSpec((C, HG_D), lambda b, h, s: (row(b, h, s), h)),
        out_shape=jax.ShapeDtypeStruct((B * S, H * HG_D), BF16),
        scratch_shapes=[pltpu.VMEM((HG_D, HG_D), F32)],
        compiler_params=_cparams(3),
        name="hgrn",
    )(z, z, z, z, lb.reshape(H, 1, HG_D), gn.reshape(H, 1, HG_D), w, lv)


def _gmlp_body(u_ref, v_ref, lng_ref, lnb_ref, ws_ref, bias_ref, o_ref, w_scr):
    @pl.when(pl.program_id(0) == 0)
    def _():
        r = lax.broadcasted_iota(jnp.int32, ws_ref.shape, 1)
        c = lax.broadcasted_iota(jnp.int32, ws_ref.shape, 2)
        w_scr[...] = jnp.where(r >= c, ws_ref[...], 0.0).astype(BF16)

    v = _gelu(v_ref[...])
    mu = jnp.mean(v, axis=-1, keepdims=True)
    vc = v - mu
    vn = vc * lax.rsqrt(jnp.mean(vc * vc, axis=-1, keepdims=True) + LN_EPS)
    vn = (vn * lng_ref[...] + lnb_ref[...]).astype(BF16)
    n_chunks = u_ref.shape[0] // GM_CHUNK
    for c in range(n_chunks):
        rows = slice(c * GM_CHUNK, (c + 1) * GM_CHUNK)
        for g in range(GM_GROUPS):
            cols = slice(g * GM_CH, (g + 1) * GM_CH)
            mixed = _dot(w_scr[g], vn[rows, cols]) + bias_ref[:, cols]
            o_ref[rows, cols] = (_gelu(u_ref[rows, cols]) * mixed).astype(BF16)


def _gmlp(z, ln_g, ln_b, w_s, b_s, tg=512):
    T = z.shape[0]
    W = GM_GROUPS * GM_CH
    u_col = 4 * HG_HEADS * HG_D // W
    bias = jnp.repeat(b_s.T, GM_CH, axis=1)
    return pl.pallas_call(
        _gmlp_body,
        grid=(T // tg,),
        in_specs=[pl.BlockSpec((tg, W), lambda i: (i, u_col)),
                  pl.BlockSpec((tg, W), lambda i: (i, u_col + 1)),
                  pl.BlockSpec((1, W), lambda i: (0, 0)),
                  pl.BlockSpec((1, W), lambda i: (0, 0)),
                  pl.BlockSpec(w_s.shape, lambda i: (0, 0, 0)),
                  pl.BlockSpec(bias.shape, lambda i: (0, 0))],
        out_specs=pl.BlockSpec((tg, W), lambda i: (i, 0)),
        out_shape=jax.ShapeDtypeStruct((T, W), BF16),
        scratch_shapes=[pltpu.VMEM(w_s.shape, BF16)],
        compiler_params=_cparams(1),
        name="gmlp",
    )(z, z, ln_g.reshape(1, W), ln_b.reshape(1, W), w_s, bias)


def _outproj_body(o_ref, sg_ref, x_ref, wa_ref, wb_ref, g_ref, wr_ref, br_ref,
                  x1_ref, h_ref, ri_ref, rg_ref, cnt_ref, tri_scr, carry_scr):
    tm = x_ref.shape[0]

    @pl.when(pl.program_id(0) == 0)
    def _():
        r = lax.broadcasted_iota(jnp.int32, (tm, tm), 0)
        c = lax.broadcasted_iota(jnp.int32, (tm, tm), 1)
        tri_scr[...] = (c < r).astype(BF16)
        carry_scr[...] = jnp.zeros_like(carry_scr)

    x1 = x_ref[...] + _dot(o_ref[...], wa_ref[...]) + _dot(sg_ref[...], wb_ref[...])
    x1_ref[...] = x1
    h = _rms(x1, g_ref[...], RMS_EPS)
    h_ref[...] = h

    logits = jnp.dot(h, wr_ref[...], preferred_element_type=F32,
                     precision=lax.Precision.HIGHEST) + br_ref[...]
    lane = lax.broadcasted_iota(jnp.int32, (tm, LANES), 1)
    neg = jnp.float32(-jnp.inf)
    work = jnp.where(lane < N_EXPERTS, logits, neg)
    vals, idxs = [], []
    for _ in range(TOP_K):
        m = jnp.max(work, axis=-1, keepdims=True)
        i = jnp.min(jnp.where(work == m, lane, LANES), axis=-1, keepdims=True)
        vals.append(m)
        idxs.append(i)
        work = jnp.where(lane == i, neg, work)
    es = [jnp.exp(m - vals[0]) for m in vals]
    denom = es[0] + es[1] + es[2] + es[3]

    onehot = jnp.zeros((tm, LANES), F32)
    for i in idxs:
        onehot = onehot + (lane == i).astype(F32)
    before = _dot(tri_scr[...], onehot.astype(BF16)) + carry_scr[0:1, :]
    carry = carry_scr[0:1, :] + jnp.sum(onehot, axis=0, keepdims=True)
    carry_scr[...] = jnp.broadcast_to(carry, carry_scr.shape)
    cnt_ref[...] = jnp.broadcast_to(carry, cnt_ref.shape).astype(jnp.int32)

    ri = jnp.zeros((tm, LANES), jnp.int32)
    rg = jnp.zeros((tm, LANES), F32)
    for kk in range(TOP_K):
        rank = jnp.sum(jnp.where(lane == idxs[kk], before, 0.0), axis=-1, keepdims=True)
        ri = jnp.where(lane == kk, idxs[kk], ri)
        ri = jnp.where(lane == TOP_K + kk, rank.astype(jnp.int32), ri)
        rg = jnp.where(lane == kk, es[kk] / denom, rg)
    ri_ref[...] = ri
    rg_ref[...] = rg


def _outproj(o, sg, x2d, w_out, g, w_router, b_router, tm=512):
    T, D = x2d.shape
    Wh = o.shape[1]
    wa = w_out[:Wh].astype(BF16)
    wb = w_out[Wh:].astype(BF16)
    wr = jnp.zeros((D, LANES), F32).at[:, :N_EXPERTS].set(w_router)
    br = jnp.zeros((1, LANES), F32).at[0, :N_EXPERTS].set(b_router)
    row = lambda i: (i, 0)
    fixed = lambda i: (0, 0)
    return pl.pallas_call(
        _outproj_body,
        grid=(T // tm,),
        in_specs=[pl.BlockSpec((tm, Wh), row), pl.BlockSpec((tm, sg.shape[1]), row),
                  pl.BlockSpec((tm, D), row),
                  pl.BlockSpec(wa.shape, fixed), pl.BlockSpec(wb.shape, fixed),
                  pl.BlockSpec((1, D), fixed), pl.BlockSpec(wr.shape, fixed),
                  pl.BlockSpec(br.shape, fixed)],
        out_specs=[pl.BlockSpec((tm, D), row), pl.BlockSpec((tm, D), row),
                   pl.BlockSpec((tm, LANES), row), pl.BlockSpec((tm, LANES), row),
                   pl.BlockSpec((8, LANES), fixed)],
        out_shape=[jax.ShapeDtypeStruct((T, D), F32), jax.ShapeDtypeStruct((T, D), F32),
                   jax.ShapeDtypeStruct((T, LANES), jnp.int32),
                   jax.ShapeDtypeStruct((T, LANES), F32),
                   jax.ShapeDtypeStruct((8, LANES), jnp.int32)],
        scratch_shapes=[pltpu.VMEM((tm, tm), BF16), pltpu.VMEM((8, LANES), F32)],
        compiler_params=_cparams(1),
        name="outproj_router",
    )(o, sg, x2d, wa, wb, g.reshape(1, D), wr, br)


def _dispatch_body(dest_ref, h_ref, xs_in_ref, xs_ref, sem):
    del xs_in_ref
    tm = h_ref.shape[0]
    base = pl.program_id(0) * (tm * TOP_K)

    def row_copy(r, d):
        return pltpu.make_async_copy(h_ref.at[pl.ds(r, 1)], xs_ref.at[pl.ds(d, 1)], sem)

    def issue(r, carry):
        for kk in range(TOP_K):
            row_copy(r, dest_ref[base + r * TOP_K + kk]).start()
        return carry

    lax.fori_loop(0, tm, issue, 0)

    def drain(r, carry):
        for kk in range(TOP_K):
            row_copy(0, 0).wait()
        return carry

    lax.fori_loop(0, tm, drain, 0)


def _dispatch(dest_flat, h, n_rows, tm=256):
    T, D = h.shape
    xs0 = jnp.zeros((n_rows, D), h.dtype)
    return pl.pallas_call(
        _dispatch_body,
        grid_spec=pltpu.PrefetchScalarGridSpec(
            num_scalar_prefetch=1,
            grid=(T // tm,),
            in_specs=[pl.BlockSpec((tm, D), lambda i, dest: (i, 0)),
                      pl.BlockSpec(memory_space=pl.ANY)],
            out_specs=pl.BlockSpec(memory_space=pl.ANY),
            scratch_shapes=[pltpu.SemaphoreType.DMA(())]),
        out_shape=jax.ShapeDtypeStruct((n_rows, D), h.dtype),
        input_output_aliases={2: 0},
        compiler_params=_cparams(1),
        name="dispatch",
    )(dest_flat, h, xs0)


def _expert_body(be_ref, nu_ref, x_ref, wg_ref, wu_ref, wd_ref, bg_ref, bu_ref, bd_ref, y_ref):
    del be_ref
    f = pl.program_id(1)
    active = pl.program_id(0) < nu_ref[0]

    @pl.when(jnp.logical_not(active))
    def _():
        y_ref[...] = jnp.zeros_like(y_ref)

    @pl.when(active)
    def _():
        x = x_ref[...].astype(BF16)
        gate = jnp.minimum(_dot(x, wg_ref[...]) + bg_ref[...], SWIGLU_LIMIT)
        up = jnp.clip(_dot(x, wu_ref[...]) + bu_ref[...], -SWIGLU_LIMIT, SWIGLU_LIMIT)
        act = (up + 1.0) * gate * jax.nn.sigmoid(SWIGLU_ALPHA * gate)
        y = _dot(act.astype(BF16), wd_ref[...])

        @pl.when(f == 0)
        def _():
            y_ref[...] = y + bd_ref[...]

        @pl.when(f > 0)
        def _():
            y_ref[...] += y


def _experts(block_expert, n_used, xs, wg, wu, wd, bg, bu, bd, fc=1024):
    n_rows, D = xs.shape
    F = wg.shape[2]
    nb = n_rows // EXPERT_BLOCK
    nf = F // fc
    blk = lambda b, nu: jnp.minimum(b, nu[0] - 1)
    fch = lambda b, f, nu: jnp.where(b < nu[0], f, nf - 1)
    return pl.pallas_call(
        _expert_body,
        grid_spec=pltpu.PrefetchScalarGridSpec(
            num_scalar_prefetch=2,
            grid=(nb, nf),
            in_specs=[
                pl.BlockSpec((EXPERT_BLOCK, D), lambda b, f, be, nu: (blk(b, nu), 0)),
                pl.BlockSpec((None, D, fc), lambda b, f, be, nu: (be[blk(b, nu)], 0, fch(b, f, nu))),
                pl.BlockSpec((None, D, fc), lambda b, f, be, nu: (be[blk(b, nu)], 0, fch(b, f, nu))),
                pl.BlockSpec((None, fc, D), lambda b, f, be, nu: (be[blk(b, nu)], fch(b, f, nu), 0)),
                pl.BlockSpec((None, 1, fc), lambda b, f, be, nu: (be[blk(b, nu)], 0, fch(b, f, nu))),
                pl.BlockSpec((None, 1, fc), lambda b, f, be, nu: (be[blk(b, nu)], 0, fch(b, f, nu))),
                pl.BlockSpec((None, 1, D), lambda b, f, be, nu: (be[blk(b, nu)], 0, 0)),
            ],
            out_specs=pl.BlockSpec((EXPERT_BLOCK, D), lambda b, f, be, nu: (b, 0))),
        out_shape=jax.ShapeDtypeStruct((n_rows, D), F32),
        compiler_params=_cparams(2),
        name="experts",
    )(block_expert, n_used, xs, wg, wu, wd, bg, bu, bd)


def _combine_body(dest_ref, x1_ref, rg_ref, p_ref, ys_ref, gp_ref, wpg_ref, wpp_ref, gf_ref,
                  out_ref, buf, sem):
    tm = x1_ref.shape[0]
    base = pl.program_id(0) * (tm * TOP_K)

    def row_copy(r, kk, d):
        return pltpu.make_async_copy(ys_ref.at[pl.ds(d, 1)], buf.at[kk, pl.ds(r, 1)], sem)

    def issue(r, carry):
        for kk in range(TOP_K):
            row_copy(r, kk, dest_ref[base + r * TOP_K + kk]).start()
        return carry

    lax.fori_loop(0, tm, issue, 0)

    def drain(r, carry):
        for kk in range(TOP_K):
            row_copy(0, kk, 0).wait()
        return carry

    lax.fori_loop(0, tm, drain, 0)

    gates = rg_ref[...]
    x2 = x1_ref[...]
    for kk in range(TOP_K):
        x2 = x2 + gates[:, kk:kk + 1] * buf[kk]
    hp = _rms(x2, gp_ref[...], RMS_EPS).astype(BF16)
    gate = jax.nn.sigmoid(_dot(hp, wpg_ref[...]))
    proj = _dot(p_ref[...].astype(BF16), wpp_ref[...])
    x3 = x2 + gate * proj
    out_ref[...] = _rms(x3, gf_ref[...], RMS_EPS)


def _combine(dest_flat, x1, rg, p2d, ys, g_ple, w_gate, w_proj, g_final, tm=256):
    T, D = x1.shape
    P = p2d.shape[1]
    row = lambda i, dest: (i, 0)
    fixed = lambda i, dest: (0, 0)
    return pl.pallas_call(
        _combine_body,
        grid_spec=pltpu.PrefetchScalarGridSpec(
            num_scalar_prefetch=1,
            grid=(T // tm,),
            in_specs=[pl.BlockSpec((tm, D), row), pl.BlockSpec((tm, LANES), row),
                      pl.BlockSpec((tm, P), row), pl.BlockSpec(memory_space=pl.ANY),
                      pl.BlockSpec((1, D), fixed), pl.BlockSpec((D, D), fixed),
                      pl.BlockSpec((P, D), fixed), pl.BlockSpec((1, D), fixed)],
            out_specs=pl.BlockSpec((tm, D), row),
            scratch_shapes=[pltpu.VMEM((TOP_K, tm, D), F32), pltpu.SemaphoreType.DMA(())]),
        out_shape=jax.ShapeDtypeStruct((T, D), F32),
        compiler_params=_cparams(1),
        name="combine_ple",
    )(dest_flat, x1, rg, p2d, ys, g_ple.reshape(1, D), w_gate.astype(BF16),
      w_proj.astype(BF16), g_final.reshape(1, D))


def _routing_tables(ri, counts_block, n_blocks):
    counts = counts_block[0, :N_EXPERTS]
    padded = (counts + EXPERT_BLOCK - 1) // EXPERT_BLOCK * EXPERT_BLOCK
    pad_ends = jnp.cumsum(padded)
    pad_starts = pad_ends - padded
    idx = ri[:, :TOP_K]
    rank = ri[:, TOP_K:2 * TOP_K]
    dest = (pad_starts[idx] + rank).reshape(-1).astype(jnp.int32)
    block_start = jnp.arange(n_blocks, dtype=jnp.int32) * EXPERT_BLOCK
    block_expert = jnp.minimum(jnp.searchsorted(pad_ends, block_start, side="right"),
                               N_EXPERTS - 1).astype(jnp.int32)
    n_used = (pad_ends[-1:] // EXPERT_BLOCK).astype(jnp.int32)
    return dest, block_expert, n_used


def _layer(x2d, p2d, B, S, norm_mix, w_in, lb, hgrn_out_norm, gmlp_ln_g, gmlp_ln_b, w_spatial,
           b_spatial, w_out, norm_ffn, w_router, b_router, w_gate_up, b_gate_up, w_down, b_down,
           norm_ple, w_ple_gate, w_ple_proj, g_out):
    T = x2d.shape[0]
    z = _inproj(x2d, norm_mix, w_in.astype(BF16))
    o = _hgrn(z, lb, hgrn_out_norm, B, S)
    sg = _gmlp(z, gmlp_ln_g, gmlp_ln_b, w_spatial, b_spatial)
    x1, h2, ri, rg, counts = _outproj(o, sg, x2d, w_out, norm_ffn, w_router, b_router)

    n_blocks = T * TOP_K // EXPERT_BLOCK + N_EXPERTS
    dest, block_expert, n_used = _routing_tables(ri, counts, n_blocks)
    xs = _dispatch(dest, h2, n_blocks * EXPERT_BLOCK)
    wg = w_gate_up[:, :, 0::2].astype(BF16)
    wu = w_gate_up[:, :, 1::2].astype(BF16)
    bg = b_gate_up[:, None, 0::2]
    bu = b_gate_up[:, None, 1::2]
    ys = _experts(block_expert, n_used, xs, wg, wu, w_down.astype(BF16), bg, bu,
                  b_down[:, None, :])
    return _combine(dest, x1, rg, p2d, ys, norm_ple, w_ple_gate, w_ple_proj, g_out)


def kernel(x, p, norm_mix, w_in, lb_logits, hgrn_out_norm, gmlp_ln_g, gmlp_ln_b, w_spatial, b_spatial, w_out, norm_ffn, w_router, b_router, w_gate_up, b_gate_up, w_down, b_down, norm_ple, w_ple_gate, w_ple_proj, norm_final):
    B, S, D = x.shape
    depth = p.shape[0]
    assert depth == 1, "the final norm is fused into the last layer's combine kernel"
    lower_bounds = jnp.cumsum(jax.nn.softmax(lb_logits.astype(F32), axis=0), axis=0)
    out = _layer(x.reshape(B * S, D), p[0].reshape(B * S, -1), B, S, norm_mix[0], w_in[0],
                 lower_bounds[0], hgrn_out_norm[0], gmlp_ln_g[0], gmlp_ln_b[0], w_spatial[0],
                 b_spatial[0], w_out[0], norm_ffn[0], w_router[0], b_router[0], w_gate_up[0],
                 b_gate_up[0], w_down[0], b_down[0], norm_ple[0], w_ple_gate[0], w_ple_proj[0],
                 norm_final)
    return out.reshape(B, S, D)
```

```python
import functools
import math

import numpy as np
import jax
import jax.numpy as jnp
from jax import lax
from jax.experimental import pallas as pl
from jax.experimental.pallas import tpu as pltpu

F32 = jnp.float32
BF16 = jnp.bfloat16

HG_HEADS = 8
HG_D = 128
GM_GROUPS = 8
GM_CH = 128
GM_CHUNK = 128
N_EXPERTS = 32
TOP_K = 4
SWIGLU_LIMIT = 7.0
SWIGLU_ALPHA = 1.702
RMS_EPS = 1e-6
LN_EPS = 1e-5

LANES = 128
HGRN_TILE = 256
EXPERT_BLOCK = 256
EXPERT_FC = 1024
VMEM_LIMIT = 56 * 1024 * 1024


def _cparams(n_axes):
    return pltpu.CompilerParams(dimension_semantics=("arbitrary",) * n_axes,
                                vmem_limit_bytes=VMEM_LIMIT)


def _rms(x, g, eps):
    return x * lax.rsqrt(jnp.mean(x * x, axis=-1, keepdims=True) + eps) * g


def _dot(a, b):
    return jnp.dot(a, b, preferred_element_type=F32)


def _dot_nt(a, b):
    return lax.dot_general(a, b, (((1,), (1,)), ((), ())), preferred_element_type=F32)


def _gelu(x):
    return 0.5 * x * (1.0 + lax.erf(x * (1.0 / math.sqrt(2.0))))


def _inproj_body(x_ref, g_ref, w_ref, z_ref, h_scr):
    @pl.when(pl.program_id(1) == 0)
    def _():
        h_scr[...] = _rms(x_ref[...], g_ref[...], RMS_EPS).astype(BF16)

    z_ref[...] = _dot(h_scr[...], w_ref[...])


def _inproj(x2d, g, w, tm=1024, tn=1024):
    T, D = x2d.shape
    N = w.shape[1]
    return pl.pallas_call(
        _inproj_body,
        grid=(T // tm, N // tn),
        in_specs=[pl.BlockSpec((tm, D), lambda i, j: (i, 0)),
                  pl.BlockSpec((1, D), lambda i, j: (0, 0)),
                  pl.BlockSpec((D, tn), lambda i, j: (0, j))],
        out_specs=pl.BlockSpec((tm, tn), lambda i, j: (i, j)),
        out_shape=jax.ShapeDtypeStruct((T, N), F32),
        scratch_shapes=[pltpu.VMEM((tm, D), BF16)],
        compiler_params=_cparams(2),
        name="inproj",
    )(x2d, g.reshape(1, D), w)


def _hgrn_constants(C):
    nlev = int(math.log2(C))
    r = np.arange(C)
    mats = [(r[None, :] <= r[:, None]), (r[None, :] > r[:, None])]
    level = np.full((C, C), -1, np.int32)
    level[r, r] = 0
    for l in range(1, nlev + 1):
        L = C >> l
        pos = r % (2 * L)
        mid = (r // (2 * L)) * (2 * L) + L - 1
        second = pos >= L
        m = np.where(second[:, None],
                     (r[None, :] > mid[:, None]) & (r[None, :] <= r[:, None]),
                     (r[None, :] > r[:, None]) & (r[None, :] <= mid[:, None]))
        mats.append(m)
        same = (r[:, None] // (2 * L)) == (r[None, :] // (2 * L))
        level[same & second[:, None] & (~second)[None, :]] = l
    w = np.concatenate(mats, axis=0).astype(np.float32)
    return jnp.asarray(w, BF16), jnp.asarray(level), nlev


def _hgrn_body(q_ref, f_ref, v_ref, og_ref, lb_ref, gn_ref, w_ref, lv_ref, o_ref, st_scr, *, nlev):
    C = q_ref.shape[0]

    @pl.when(pl.program_id(2) == 0)
    def _():
        st_scr[...] = jnp.zeros_like(st_scr)

    q = q_ref[...]
    q = q * jax.nn.sigmoid(q)
    lb = lb_ref[...]
    f = lb + (1.0 - lb) * jax.nn.sigmoid(f_ref[...])
    g = jnp.log(f)
    k = 1.0 - f
    v = v_ref[...]

    g1 = g.astype(BF16)
    r1 = g - g1.astype(F32)
    g2 = r1.astype(BF16)
    g3 = (r1 - g2.astype(F32)).astype(BF16)
    gcat = jnp.concatenate([g1, g2, g3], axis=1)

    def range_sum(i):
        e3 = _dot(w_ref[i * C:(i + 1) * C, :], gcat)
        return e3[:, :HG_D] + e3[:, HG_D:2 * HG_D] + e3[:, 2 * HG_D:]

    b = range_sum(0)
    suffix = range_sum(1)
    lv = lv_ref[...]

    a = jnp.where(lv == 0, _dot_nt(q.astype(BF16), k.astype(BF16)), 0.0)
    for l in range(1, nlev + 1):
        e = jnp.exp(range_sum(1 + l))
        a = jnp.where(lv == l, _dot_nt((q * e).astype(BF16), (k * e).astype(BF16)), a)

    st = st_scr[...]
    o = _dot_nt((q * jnp.exp(b)).astype(BF16), st.astype(BF16))
    o = o + _dot(a.astype(BF16), v.astype(BF16))

    khat = (k * jnp.exp(suffix)).astype(BF16)
    st_scr[...] = st * jnp.exp(b[C - 1:C, :]) + _dot(v.T.astype(BF16), khat)

    o = _rms(o, gn_ref[...], RMS_EPS)
    o_ref[...] = (o * jax.nn.sigmoid(og_ref[...])).astype(BF16)


def _hgrn(z, lb, gn, B, S):
    C = HGRN_TILE
    H = HG_HEADS
    n_s = S // C
    w, lv, nlev = _hgrn_constants(C)
    row = lambda b, h, s: b * n_s + s
    col = lambda off: pl.BlockSpec((C, HG_D), lambda b, h, s: (row(b, h, s), off * H + h))
    per_head = pl.BlockSpec((None, 1, HG_D), lambda b, h, s: (h, 0, 0))
    return pl.pallas_call(
        functools.partial(_hgrn_body, nlev=nlev),
        grid=(B, H, n_s),
        in_specs=[col(0), col(1), col(2), col(3), per_head, per_head,
                  pl.BlockSpec(w.shape, lambda b, h, s: (0, 0)),
                  pl.BlockSpec(lv.shape, lambda b, h, s: (0, 0))],
        out_specs=pl.BlockSpec((C, HG_D), lambda b, h, s: (row(b, h, s), h)),
        out_shape=jax.ShapeDtypeStruct((B * S, H * HG_D), BF16),
        scratch_shapes=[pltpu.VMEM((HG_D, HG_D), F32)],
        compiler_params=_cparams(3),
        name="hgrn",
    )(z, z, z, z, lb.reshape(H, 1, HG_D), gn.reshape(H, 1, HG_D), w, lv)


def _gmlp_body(u_ref, v_ref, lng_ref, lnb_ref, ws_ref, bias_ref, o_ref, w_scr):
    @pl.when(pl.program_id(0) == 0)
    def _():
        r = lax.broadcasted_iota(jnp.int32, ws_ref.shape, 1)
        c = lax.broadcasted_iota(jnp.int32, ws_ref.shape, 2)
        w_scr[...] = jnp.where(r >= c, ws_ref[...], 0.0).astype(BF16)

    v = _gelu(v_ref[...])
    mu = jnp.mean(v, axis=-1, keepdims=True)
    vc = v - mu
    vn = vc * lax.rsqrt(jnp.mean(vc * vc, axis=-1, keepdims=True) + LN_EPS)
    vn = (vn * lng_ref[...] + lnb_ref[...]).astype(BF16)
    n_chunks = u_ref.shape[0] // GM_CHUNK
    for c in range(n_chunks):
        rows = slice(c * GM_CHUNK, (c + 1) * GM_CHUNK)
        for g in range(GM_GROUPS):
            cols = slice(g * GM_CH, (g + 1) * GM_CH)
            mixed = _dot(w_scr[g], vn[rows, cols]) + bias_ref[:, cols]
            o_ref[rows, cols] = (_gelu(u_ref[rows, cols]) * mixed).astype(BF16)


def _gmlp(z, ln_g, ln_b, w_s, b_s, tg=512):
    T = z.shape[0]
    W = GM_GROUPS * GM_CH
    u_col = 4 * HG_HEADS * HG_D // W
    bias = jnp.repeat(b_s.T, GM_CH, axis=1)
    return pl.pallas_call(
        _gmlp_body,
        grid=(T // tg,),
        in_specs=[pl.BlockSpec((tg, W), lambda i: (i, u_col)),
                  pl.BlockSpec((tg, W), lambda i: (i, u_col + 1)),
                  pl.BlockSpec((1, W), lambda i: (0, 0)),
                  pl.BlockSpec((1, W), lambda i: (0, 0)),
                  pl.BlockSpec(w_s.shape, lambda i: (0, 0, 0)),
                  pl.BlockSpec(bias.shape, lambda i: (0, 0))],
        out_specs=pl.BlockSpec((tg, W), lambda i: (i, 0)),
        out_shape=jax.ShapeDtypeStruct((T, W), BF16),
        scratch_shapes=[pltpu.VMEM(w_s.shape, BF16)],
        compiler_params=_cparams(1),
        name="gmlp",
    )(z, z, ln_g.reshape(1, W), ln_b.reshape(1, W), w_s, bias)


def _outproj_body(o_ref, sg_ref, x_ref, wa_ref, wb_ref, g_ref, wr_ref, br_ref,
                  x1_ref, h_ref, ri_ref, rg_ref, cnt_ref, tri_scr, carry_scr):
    tm = x_ref.shape[0]

    @pl.when(pl.program_id(0) == 0)
    def _():
        r = lax.broadcasted_iota(jnp.int32, (tm, tm), 0)
        c = lax.broadcasted_iota(jnp.int32, (tm, tm), 1)
        tri_scr[...] = (c < r).astype(BF16)
        carry_scr[...] = jnp.zeros_like(carry_scr)

    x1 = x_ref[...] + _dot(o_ref[...], wa_ref[...]) + _dot(sg_ref[...], wb_ref[...])
    x1_ref[...] = x1
    h = _rms(x1, g_ref[...], RMS_EPS)
    h_ref[...] = h

    logits = jnp.dot(h, wr_ref[...], preferred_element_type=F32,
                     precision=lax.Precision.HIGHEST) + br_ref[...]
    lane = lax.broadcasted_iota(jnp.int32, (tm, LANES), 1)
    neg = jnp.float32(-jnp.inf)
    work = jnp.where(lane < N_EXPERTS, logits, neg)
    vals, idxs = [], []
    for _ in range(TOP_K):
        m = jnp.max(work, axis=-1, keepdims=True)
        i = jnp.min(jnp.where(work == m, lane, LANES), axis=-1, keepdims=True)
        vals.append(m)
        idxs.append(i)
        work = jnp.where(lane == i, neg, work)
    es = [jnp.exp(m - vals[0]) for m in vals]
    denom = es[0] + es[1] + es[2] + es[3]

    onehot = jnp.zeros((tm, LANES), F32)
    for i in idxs:
        onehot = onehot + (lane == i).astype(F32)
    before = _dot(tri_scr[...], onehot.astype(BF16)) + carry_scr[0:1, :]
    carry = carry_scr[0:1, :] + jnp.sum(onehot, axis=0, keepdims=True)
    carry_scr[...] = jnp.broadcast_to(carry, carry_scr.shape)
    cnt_ref[...] = jnp.broadcast_to(carry, cnt_ref.shape).astype(jnp.int32)

    ri = jnp.zeros((tm, LANES), jnp.int32)
    rg = jnp.zeros((tm, LANES), F32)
    for kk in range(TOP_K):
        rank = jnp.sum(jnp.where(lane == idxs[kk], before, 0.0), axis=-1, keepdims=True)
        ri = jnp.where(lane == kk, idxs[kk], ri)
        ri = jnp.where(lane == TOP_K + kk, rank.astype(jnp.int32), ri)
        rg = jnp.where(lane == kk, es[kk] / denom, rg)
    ri_ref[...] = ri
    rg_ref[...] = rg


def _outproj(o, sg, x2d, w_out, g, w_router, b_router, tm=512):
    T, D = x2d.shape
    Wh = o.shape[1]
    wa = w_out[:Wh].astype(BF16)
    wb = w_out[Wh:].astype(BF16)
    wr = jnp.zeros((D, LANES), F32).at[:, :N_EXPERTS].set(w_router)
    br = jnp.zeros((1, LANES), F32).at[0, :N_EXPERTS].set(b_router)
    row = lambda i: (i, 0)
    fixed = lambda i: (0, 0)
    return pl.pallas_call(
        _outproj_body,
        grid=(T // tm,),
        in_specs=[pl.BlockSpec((tm, Wh), row), pl.BlockSpec((tm, sg.shape[1]), row),
                  pl.BlockSpec((tm, D), row),
                  pl.BlockSpec(wa.shape, fixed), pl.BlockSpec(wb.shape, fixed),
                  pl.BlockSpec((1, D), fixed), pl.BlockSpec(wr.shape, fixed),
                  pl.BlockSpec(br.shape, fixed)],
        out_specs=[pl.BlockSpec((tm, D), row), pl.BlockSpec((tm, D), row),
                   pl.BlockSpec((tm, LANES), row), pl.BlockSpec((tm, LANES), row),
                   pl.BlockSpec((8, LANES), fixed)],
        out_shape=[jax.ShapeDtypeStruct((T, D), F32), jax.ShapeDtypeStruct((T, D), F32),
                   jax.ShapeDtypeStruct((T, LANES), jnp.int32),
                   jax.ShapeDtypeStruct((T, LANES), F32),
                   jax.ShapeDtypeStruct((8, LANES), jnp.int32)],
        scratch_shapes=[pltpu.VMEM((tm, tm), BF16), pltpu.VMEM((8, LANES), F32)],
        compiler_params=_cparams(1),
        name="outproj_router",
    )(o, sg, x2d, wa, wb, g.reshape(1, D), wr, br)


def _dispatch_body(dest_ref, h_ref, xs_in_ref, xs_ref, sem):
    del xs_in_ref
    tm = h_ref.shape[0]
    base = pl.program_id(0) * (tm * TOP_K)

    def row_copy(r, d):
        return pltpu.make_async_copy(h_ref.at[pl.ds(r, 1)], xs_ref.at[pl.ds(d, 1)], sem)

    def issue(r, carry):
        for kk in range(TOP_K):
            row_copy(r, dest_ref[base + r * TOP_K + kk]).start()
        return carry

    lax.fori_loop(0, tm, issue, 0)

    def drain(r, carry):
        for kk in range(TOP_K):
            row_copy(0, 0).wait()
        return carry

    lax.fori_loop(0, tm, drain, 0)


def _dispatch(dest_flat, h, n_rows, tm=256):
    T, D = h.shape
    xs0 = jnp.zeros((n_rows, D), h.dtype)
    return pl.pallas_call(
        _dispatch_body,
        grid_spec=pltpu.PrefetchScalarGridSpec(
            num_scalar_prefetch=1,
            grid=(T // tm,),
            in_specs=[pl.BlockSpec((tm, D), lambda i, dest: (i, 0)),
                      pl.BlockSpec(memory_space=pl.ANY)],
            out_specs=pl.BlockSpec(memory_space=pl.ANY),
            scratch_shapes=[pltpu.SemaphoreType.DMA(())]),
        out_shape=jax.ShapeDtypeStruct((n_rows, D), h.dtype),
        input_output_aliases={2: 0},
        compiler_params=_cparams(1),
        name="dispatch",
    )(dest_flat, h, xs0)


def _swiglu_deinterleaved(gu, fc):
    even = lax.broadcasted_iota(jnp.int32, (gu.shape[0], LANES), 1) % 2 == 0
    acts = []
    for c in range(fc // LANES):
        a = gu[:, c * LANES:(c + 1) * LANES]
        b = gu[:, fc + c * LANES:fc + (c + 1) * LANES]
        gate = jnp.where(even, a, pltpu.roll(b, 1, axis=1))
        up = jnp.where(even, pltpu.roll(a, LANES - 1, axis=1), b)
        gate = jnp.minimum(gate, SWIGLU_LIMIT)
        up = jnp.clip(up, -SWIGLU_LIMIT, SWIGLU_LIMIT)
        acts.append(((up + 1.0) * gate * jax.nn.sigmoid(SWIGLU_ALPHA * gate)).astype(BF16))
    return jnp.concatenate(acts, axis=1)


def _permute_down_rows(w_down, fc):
    E, F, D = w_down.shape
    w = w_down.reshape(E, F // fc, 2, fc // LANES, LANES // 2, D)
    return w.transpose(0, 1, 3, 4, 2, 5).reshape(E, F, D)


def _expert_body(be_ref, nu_ref, x_ref, wgu_ref, wd_ref, bgu_ref, bd_ref, y_ref):
    del be_ref
    f = pl.program_id(1)
    fc = wd_ref.shape[0]
    active = pl.program_id(0) < nu_ref[0]

    @pl.when(jnp.logical_not(active))
    def _():
        y_ref[...] = jnp.zeros_like(y_ref)

    @pl.when(active)
    def _():
        x = x_ref[...].astype(BF16)
        act = _swiglu_deinterleaved(_dot(x, wgu_ref[...]) + bgu_ref[...], fc)
        y = _dot(act, wd_ref[...])

        @pl.when(f == 0)
        def _():
            y_ref[...] = y + bd_ref[...]

        @pl.when(f > 0)
        def _():
            y_ref[...] += y


def _experts(block_expert, n_used, xs, wgu, wd, bgu, bd, fc):
    n_rows, D = xs.shape
    F = wd.shape[1]
    nb = n_rows // EXPERT_BLOCK
    nf = F // fc
    blk = lambda b, nu: jnp.minimum(b, nu[0] - 1)
    fch = lambda b, f, nu: jnp.where(b < nu[0], f, nf - 1)
    return pl.pallas_call(
        _expert_body,
        grid_spec=pltpu.PrefetchScalarGridSpec(
            num_scalar_prefetch=2,
            grid=(nb, nf),
            in_specs=[
                pl.BlockSpec((EXPERT_BLOCK, D), lambda b, f, be, nu: (blk(b, nu), 0)),
                pl.BlockSpec((None, D, 2 * fc),
                             lambda b, f, be, nu: (be[blk(b, nu)], 0, fch(b, f, nu))),
                pl.BlockSpec((None, fc, D),
                             lambda b, f, be, nu: (be[blk(b, nu)], fch(b, f, nu), 0)),
                pl.BlockSpec((None, 1, 2 * fc),
                             lambda b, f, be, nu: (be[blk(b, nu)], 0, fch(b, f, nu))),
                pl.BlockSpec((None, 1, D), lambda b, f, be, nu: (be[blk(b, nu)], 0, 0)),
            ],
            out_specs=pl.BlockSpec((EXPERT_BLOCK, D), lambda b, f, be, nu: (b, 0))),
        out_shape=jax.ShapeDtypeStruct((n_rows, D), F32),
        compiler_params=_cparams(2),
        name="experts",
    )(block_expert, n_used, xs, wgu, wd, bgu, bd)


def _combine_body(dest_ref, x1_ref, rg_ref, p_ref, ys_ref, gp_ref, wpg_ref, wpp_ref, gf_ref,
                  out_ref, buf, sem):
    tm = x1_ref.shape[0]
    base = pl.program_id(0) * (tm * TOP_K)

    def row_copy(r, kk, d):
        return pltpu.make_async_copy(ys_ref.at[pl.ds(d, 1)], buf.at[kk, pl.ds(r, 1)], sem)

    def issue(r, carry):
        for kk in range(TOP_K):
            row_copy(r, kk, dest_ref[base + r * TOP_K + kk]).start()
        return carry

    lax.fori_loop(0, tm, issue, 0)

    def drain(r, carry):
        for kk in range(TOP_K):
            row_copy(0, kk, 0).wait()
        return carry

    lax.fori_loop(0, tm, drain, 0)

    gates = rg_ref[...]
    x2 = x1_ref[...]
    for kk in range(TOP_K):
        x2 = x2 + gates[:, kk:kk + 1] * buf[kk]
    hp = _rms(x2, gp_ref[...], RMS_EPS).astype(BF16)
    gate = jax.nn.sigmoid(_dot(hp, wpg_ref[...]))
    proj = _dot(p_ref[...].astype(BF16), wpp_ref[...])
    x3 = x2 + gate * proj
    out_ref[...] = _rms(x3, gf_ref[...], RMS_EPS)


def _combine(dest_flat, x1, rg, p2d, ys, g_ple, w_gate, w_proj, g_final, tm=256):
    T, D = x1.shape
    P = p2d.shape[1]
    row = lambda i, dest: (i, 0)
    fixed = lambda i, dest: (0, 0)
    return pl.pallas_call(
        _combine_body,
        grid_spec=pltpu.PrefetchScalarGridSpec(
            num_scalar_prefetch=1,
            grid=(T // tm,),
            in_specs=[pl.BlockSpec((tm, D), row), pl.BlockSpec((tm, LANES), row),
                      pl.BlockSpec((tm, P), row), pl.BlockSpec(memory_space=pl.ANY),
                      pl.BlockSpec((1, D), fixed), pl.BlockSpec((D, D), fixed),
                      pl.BlockSpec((P, D), fixed), pl.BlockSpec((1, D), fixed)],
            out_specs=pl.BlockSpec((tm, D), row),
            scratch_shapes=[pltpu.VMEM((TOP_K, tm, D), F32), pltpu.SemaphoreType.DMA(())]),
        out_shape=jax.ShapeDtypeStruct((T, D), F32),
        compiler_params=_cparams(1),
        name="combine_ple",
    )(dest_flat, x1, rg, p2d, ys, g_ple.reshape(1, D), w_gate.astype(BF16),
      w_proj.astype(BF16), g_final.reshape(1, D))


def _routing_tables(ri, counts_block, n_blocks):
    counts = counts_block[0, :N_EXPERTS]
    padded = (counts + EXPERT_BLOCK - 1) // EXPERT_BLOCK * EXPERT_BLOCK
    pad_ends = jnp.cumsum(padded)
    pad_starts = pad_ends - padded
    idx = ri[:, :TOP_K]
    rank = ri[:, TOP_K:2 * TOP_K]
    dest = (pad_starts[idx] + rank).reshape(-1).astype(jnp.int32)
    block_start = jnp.arange(n_blocks, dtype=jnp.int32) * EXPERT_BLOCK
    block_expert = jnp.minimum(jnp.searchsorted(pad_ends, block_start, side="right"),
                               N_EXPERTS - 1).astype(jnp.int32)
    n_used = (pad_ends[-1:] // EXPERT_BLOCK).astype(jnp.int32)
    return dest, block_expert, n_used


def _layer(x2d, p2d, B, S, norm_mix, w_in, lb, hgrn_out_norm, gmlp_ln_g, gmlp_ln_b, w_spatial,
           b_spatial, w_out, norm_ffn, w_router, b_router, w_gate_up, b_gate_up, w_down, b_down,
           norm_ple, w_ple_gate, w_ple_proj, g_out):
    T = x2d.shape[0]
    z = _inproj(x2d, norm_mix, w_in.astype(BF16))
    o = _hgrn(z, lb, hgrn_out_norm, B, S)
    sg = _gmlp(z, gmlp_ln_g, gmlp_ln_b, w_spatial, b_spatial)
    x1, h2, ri, rg, counts = _outproj(o, sg, x2d, w_out, norm_ffn, w_router, b_router)

    n_blocks = T * TOP_K // EXPERT_BLOCK + N_EXPERTS
    dest, block_expert, n_used = _routing_tables(ri, counts, n_blocks)
    xs = _dispatch(dest, h2, n_blocks * EXPERT_BLOCK)
    ys = _experts(block_expert, n_used, xs, w_gate_up.astype(BF16),
                  _permute_down_rows(w_down, EXPERT_FC).astype(BF16), b_gate_up[:, None, :],
                  b_down[:, None, :], EXPERT_FC)
    return _combine(dest, x1, rg, p2d, ys, norm_ple, w_ple_gate, w_ple_proj, g_out)


def kernel(x, p, norm_mix, w_in, lb_logits, hgrn_out_norm, gmlp_ln_g, gmlp_ln_b, w_spatial, b_spatial, w_out, norm_ffn, w_router, b_router, w_gate_up, b_gate_up, w_down, b_down, norm_ple, w_ple_gate, w_ple_proj, norm_final):
    B, S, D = x.shape
    depth = p.shape[0]
    assert depth == 1, "the final norm is fused into the last layer's combine kernel"
    lower_bounds = jnp.cumsum(jax.nn.softmax(lb_logits.astype(F32), axis=0), axis=0)
    out = _layer(x.reshape(B * S, D), p[0].reshape(B * S, -1), B, S, norm_mix[0], w_in[0],
                 lower_bounds[0], hgrn_out_norm[0], gmlp_ln_g[0], gmlp_ln_b[0], w_spatial[0],
                 b_spatial[0], w_out[0], norm_ffn[0], w_router[0], b_router[0], w_gate_up[0],
                 b_gate_up[0], w_down[0], b_down[0], norm_ple[0], w_ple_gate[0], w_ple_proj[0],
                 norm_final)
    return out.reshape(B, S, D)
```

```python
import functools
import math

import numpy as np
import jax
import jax.numpy as jnp
from jax import lax
from jax.experimental import pallas as pl
from jax.experimental.pallas import tpu as pltpu

F32 = jnp.float32
BF16 = jnp.bfloat16

HG_HEADS = 8
HG_D = 128
GM_GROUPS = 8
GM_CH = 128
GM_CHUNK = 128
N_EXPERTS = 32
TOP_K = 4
SWIGLU_LIMIT = 7.0
SWIGLU_ALPHA = 1.702
RMS_EPS = 1e-6
LN_EPS = 1e-5

LANES = 128
HGRN_TILE = 256
EXPERT_BLOCK = 256
EXPERT_ROWS = 2048
EXPERT_FC = 512
EXPERT_NC = 512
VMEM_LIMIT = 56 * 1024 * 1024


def _cparams(n_axes):
    return pltpu.CompilerParams(dimension_semantics=("arbitrary",) * n_axes,
                                vmem_limit_bytes=VMEM_LIMIT)


def _rms(x, g, eps):
    return x * lax.rsqrt(jnp.mean(x * x, axis=-1, keepdims=True) + eps) * g


def _dot(a, b):
    return jnp.dot(a, b, preferred_element_type=F32)


def _dot_nt(a, b):
    return lax.dot_general(a, b, (((1,), (1,)), ((), ())), preferred_element_type=F32)


def _bf16_bits(x):
    return pltpu.bitcast(x.astype(BF16).astype(F32), jnp.uint32) & jnp.uint32(0xFFFF0000)


def _pack_bf16_pair(lo, hi):
    return (_bf16_bits(lo) >> 16) | _bf16_bits(hi)


def _unpack_bf16_pair(w):
    lo = pltpu.bitcast(w << 16, F32).astype(BF16)
    hi = pltpu.bitcast(w & jnp.uint32(0xFFFF0000), F32).astype(BF16)
    return lo, hi


def _gelu(x):
    return 0.5 * x * (1.0 + lax.erf(x * (1.0 / math.sqrt(2.0))))


def _inproj_body(x_ref, g_ref, w_ref, z_ref, h_scr):
    @pl.when(pl.program_id(1) == 0)
    def _():
        h_scr[...] = _rms(x_ref[...], g_ref[...], RMS_EPS).astype(BF16)

    z_ref[...] = _dot(h_scr[...], w_ref[...])


def _inproj(x2d, g, w, tm=1024, tn=1024):
    T, D = x2d.shape
    N = w.shape[1]
    return pl.pallas_call(
        _inproj_body,
        grid=(T // tm, N // tn),
        in_specs=[pl.BlockSpec((tm, D), lambda i, j: (i, 0)),
                  pl.BlockSpec((1, D), lambda i, j: (0, 0)),
                  pl.BlockSpec((D, tn), lambda i, j: (0, j))],
        out_specs=pl.BlockSpec((tm, tn), lambda i, j: (i, j)),
        out_shape=jax.ShapeDtypeStruct((T, N), F32),
        scratch_shapes=[pltpu.VMEM((tm, D), BF16)],
        compiler_params=_cparams(2),
        name="inproj",
    )(x2d, g.reshape(1, D), w)


def _hgrn_constants(C):
    nlev = int(math.log2(C))
    r = np.arange(C)
    mats = [(r[None, :] <= r[:, None]), (r[None, :] > r[:, None])]
    level = np.full((C, C), -1, np.int32)
    level[r, r] = 0
    for l in range(1, nlev + 1):
        L = C >> l
        pos = r % (2 * L)
        mid = (r // (2 * L)) * (2 * L) + L - 1
        second = pos >= L
        m = np.where(second[:, None],
                     (r[None, :] > mid[:, None]) & (r[None, :] <= r[:, None]),
                     (r[None, :] > r[:, None]) & (r[None, :] <= mid[:, None]))
        mats.append(m)
        same = (r[:, None] // (2 * L)) == (r[None, :] // (2 * L))
        level[same & second[:, None] & (~second)[None, :]] = l
    w = np.concatenate(mats, axis=0).astype(np.float32)
    return jnp.asarray(w, BF16), jnp.asarray(level), nlev


def _hgrn_body(q_ref, f_ref, v_ref, og_ref, lb_ref, gn_ref, w_ref, lv_ref, o_ref, st_scr, *, nlev):
    C = q_ref.shape[0]

    @pl.when(pl.program_id(2) == 0)
    def _():
        st_scr[...] = jnp.zeros_like(st_scr)

    q = q_ref[...]
    q = q * jax.nn.sigmoid(q)
    lb = lb_ref[...]
    f = lb + (1.0 - lb) * jax.nn.sigmoid(f_ref[...])
    g = jnp.log(f)
    k = 1.0 - f
    v = v_ref[...]

    g1 = g.astype(BF16)
    r1 = g - g1.astype(F32)
    g2 = r1.astype(BF16)
    g3 = (r1 - g2.astype(F32)).astype(BF16)
    gcat = jnp.concatenate([g1, g2, g3], axis=1)

    def range_sum(i):
        e3 = _dot(w_ref[i * C:(i + 1) * C, :], gcat)
        return e3[:, :HG_D] + e3[:, HG_D:2 * HG_D] + e3[:, 2 * HG_D:]

    b = range_sum(0)
    suffix = range_sum(1)
    lv = lv_ref[...]

    a = jnp.where(lv == 0, _dot_nt(q.astype(BF16), k.astype(BF16)), 0.0)
    for l in range(1, nlev + 1):
        e = jnp.exp(range_sum(1 + l))
        a = jnp.where(lv == l, _dot_nt((q * e).astype(BF16), (k * e).astype(BF16)), a)

    st = st_scr[...]
    o = _dot_nt((q * jnp.exp(b)).astype(BF16), st.astype(BF16))
    o = o + _dot(a.astype(BF16), v.astype(BF16))

    khat = (k * jnp.exp(suffix)).astype(BF16)
    st_scr[...] = st * jnp.exp(b[C - 1:C, :]) + _dot(v.T.astype(BF16), khat)

    o = _rms(o, gn_ref[...], RMS_EPS)
    o_ref[...] = (o * jax.nn.sigmoid(og_ref[...])).astype(BF16)


def _hgrn(z, lb, gn, B, S):
    C = HGRN_TILE
    H = HG_HEADS
    n_s = S // C
    w, lv, nlev = _hgrn_constants(C)
    row = lambda b, h, s: b * n_s + s
    col = lambda off: pl.BlockSpec((C, HG_D), lambda b, h, s: (row(b, h, s), off * H + h))
    per_head = pl.BlockSpec((None, 1, HG_D), lambda b, h, s: (h, 0, 0))
    return pl.pallas_call(
        functools.partial(_hgrn_body, nlev=nlev),
        grid=(B, H, n_s),
        in_specs=[col(0), col(1), col(2), col(3), per_head, per_head,
                  pl.BlockSpec(w.shape, lambda b, h, s: (0, 0)),
                  pl.BlockSpec(lv.shape, lambda b, h, s: (0, 0))],
        out_specs=pl.BlockSpec((C, HG_D), lambda b, h, s: (row(b, h, s), h)),
        out_shape=jax.ShapeDtypeStruct((B * S, H * HG_D), BF16),
        scratch_shapes=[pltpu.VMEM((HG_D, HG_D), F32)],
        compiler_params=_cparams(3),
        name="hgrn",
    )(z, z, z, z, lb.reshape(H, 1, HG_D), gn.reshape(H, 1, HG_D), w, lv)


def _gmlp_body(u_ref, v_ref, lng_ref, lnb_ref, ws_ref, bias_ref, o_ref, w_scr):
    @pl.when(pl.program_id(0) == 0)
    def _():
        r = lax.broadcasted_iota(jnp.int32, ws_ref.shape, 1)
        c = lax.broadcasted_iota(jnp.int32, ws_ref.shape, 2)
        w_scr[...] = jnp.where(r >= c, ws_ref[...], 0.0).astype(BF16)

    v = _gelu(v_ref[...])
    mu = jnp.mean(v, axis=-1, keepdims=True)
    vc = v - mu
    vn = vc * lax.rsqrt(jnp.mean(vc * vc, axis=-1, keepdims=True) + LN_EPS)
    vn = (vn * lng_ref[...] + lnb_ref[...]).astype(BF16)
    n_chunks = u_ref.shape[0] // GM_CHUNK
    for c in range(n_chunks):
        rows = slice(c * GM_CHUNK, (c + 1) * GM_CHUNK)
        for g in range(GM_GROUPS):
            cols = slice(g * GM_CH, (g + 1) * GM_CH)
            mixed = _dot(w_scr[g], vn[rows, cols]) + bias_ref[:, cols]
            o_ref[rows, cols] = (_gelu(u_ref[rows, cols]) * mixed).astype(BF16)


def _gmlp(z, ln_g, ln_b, w_s, b_s, tg=512):
    T = z.shape[0]
    W = GM_GROUPS * GM_CH
    u_col = 4 * HG_HEADS * HG_D // W
    bias = jnp.repeat(b_s.T, GM_CH, axis=1)
    return pl.pallas_call(
        _gmlp_body,
        grid=(T // tg,),
        in_specs=[pl.BlockSpec((tg, W), lambda i: (i, u_col)),
                  pl.BlockSpec((tg, W), lambda i: (i, u_col + 1)),
                  pl.BlockSpec((1, W), lambda i: (0, 0)),
                  pl.BlockSpec((1, W), lambda i: (0, 0)),
                  pl.BlockSpec(w_s.shape, lambda i: (0, 0, 0)),
                  pl.BlockSpec(bias.shape, lambda i: (0, 0))],
        out_specs=pl.BlockSpec((tg, W), lambda i: (i, 0)),
        out_shape=jax.ShapeDtypeStruct((T, W), BF16),
        scratch_shapes=[pltpu.VMEM(w_s.shape, BF16)],
        compiler_params=_cparams(1),
        name="gmlp",
    )(z, z, ln_g.reshape(1, W), ln_b.reshape(1, W), w_s, bias)


def _outproj_body(o_ref, sg_ref, x_ref, wa_ref, wb_ref, g_ref, wr_ref, br_ref,
                  x1_ref, h_ref, ri_ref, rg_ref, cnt_ref, tri_scr, carry_scr):
    tm = x_ref.shape[0]

    @pl.when(pl.program_id(0) == 0)
    def _():
        r = lax.broadcasted_iota(jnp.int32, (tm, tm), 0)
        c = lax.broadcasted_iota(jnp.int32, (tm, tm), 1)
        tri_scr[...] = (c < r).astype(BF16)
        carry_scr[...] = jnp.zeros_like(carry_scr)

    x1 = x_ref[...] + _dot(o_ref[...], wa_ref[...]) + _dot(sg_ref[...], wb_ref[...])
    x1_ref[...] = x1
    h = _rms(x1, g_ref[...], RMS_EPS)
    half = h.shape[1] // 2
    h_ref[...] = _pack_bf16_pair(h[:, :half], h[:, half:])

    logits = jnp.dot(h, wr_ref[...], preferred_element_type=F32,
                     precision=lax.Precision.HIGHEST) + br_ref[...]
    lane = lax.broadcasted_iota(jnp.int32, (tm, LANES), 1)
    neg = jnp.float32(-jnp.inf)
    work = jnp.where(lane < N_EXPERTS, logits, neg)
    vals, idxs = [], []
    for _ in range(TOP_K):
        m = jnp.max(work, axis=-1, keepdims=True)
        i = jnp.min(jnp.where(work == m, lane, LANES), axis=-1, keepdims=True)
        vals.append(m)
        idxs.append(i)
        work = jnp.where(lane == i, neg, work)
    es = [jnp.exp(m - vals[0]) for m in vals]
    denom = es[0] + es[1] + es[2] + es[3]

    onehot = jnp.zeros((tm, LANES), F32)
    for i in idxs:
        onehot = onehot + (lane == i).astype(F32)
    before = _dot(tri_scr[...], onehot.astype(BF16)) + carry_scr[0:1, :]
    carry = carry_scr[0:1, :] + jnp.sum(onehot, axis=0, keepdims=True)
    carry_scr[...] = jnp.broadcast_to(carry, carry_scr.shape)
    cnt_ref[...] = jnp.broadcast_to(carry, cnt_ref.shape).astype(jnp.int32)

    ri = jnp.zeros((tm, LANES), jnp.int32)
    rg = jnp.zeros((tm, LANES), F32)
    for kk in range(TOP_K):
        rank = jnp.sum(jnp.where(lane == idxs[kk], before, 0.0), axis=-1, keepdims=True)
        ri = jnp.where(lane == kk, idxs[kk], ri)
        ri = jnp.where(lane == TOP_K + kk, rank.astype(jnp.int32), ri)
        rg = jnp.where(lane == kk, es[kk] / denom, rg)
    ri_ref[...] = ri
    rg_ref[...] = rg


def _outproj(o, sg, x2d, w_out, g, w_router, b_router, tm=512):
    T, D = x2d.shape
    Wh = o.shape[1]
    wa = w_out[:Wh].astype(BF16)
    wb = w_out[Wh:].astype(BF16)
    wr = jnp.zeros((D, LANES), F32).at[:, :N_EXPERTS].set(w_router)
    br = jnp.zeros((1, LANES), F32).at[0, :N_EXPERTS].set(b_router)
    row = lambda i: (i, 0)
    fixed = lambda i: (0, 0)
    return pl.pallas_call(
        _outproj_body,
        grid=(T // tm,),
        in_specs=[pl.BlockSpec((tm, Wh), row), pl.BlockSpec((tm, sg.shape[1]), row),
                  pl.BlockSpec((tm, D), row),
                  pl.BlockSpec(wa.shape, fixed), pl.BlockSpec(wb.shape, fixed),
                  pl.BlockSpec((1, D), fixed), pl.BlockSpec(wr.shape, fixed),
                  pl.BlockSpec(br.shape, fixed)],
        out_specs=[pl.BlockSpec((tm, D), row), pl.BlockSpec((tm, D // 2), row),
                   pl.BlockSpec((tm, LANES), row), pl.BlockSpec((tm, LANES), row),
                   pl.BlockSpec((8, LANES), fixed)],
        out_shape=[jax.ShapeDtypeStruct((T, D), F32), jax.ShapeDtypeStruct((T, D // 2), jnp.uint32),
                   jax.ShapeDtypeStruct((T, LANES), jnp.int32),
                   jax.ShapeDtypeStruct((T, LANES), F32),
                   jax.ShapeDtypeStruct((8, LANES), jnp.int32)],
        scratch_shapes=[pltpu.VMEM((tm, tm), BF16), pltpu.VMEM((8, LANES), F32)],
        compiler_params=_cparams(1),
        name="outproj_router",
    )(o, sg, x2d, wa, wb, g.reshape(1, D), wr, br)


def _dispatch_body(dest_ref, ends_ref, h_ref, xs_ref, zero_scr, sem, zsem):
    tm = h_ref.shape[0]
    base = pl.program_id(0) * (tm * TOP_K)

    @pl.when(pl.program_id(0) == 0)
    def _():
        zero_scr[...] = jnp.zeros_like(zero_scr)

        def last_block(e, op):
            end = ends_ref[e]
            prev = ends_ref[e - 1] if e else 0

            @pl.when(end > prev)
            def _():
                start = pl.multiple_of(end - EXPERT_BLOCK, EXPERT_BLOCK)
                op(pltpu.make_async_copy(zero_scr, xs_ref.at[pl.ds(start, EXPERT_BLOCK)], zsem))

        for e in range(N_EXPERTS):
            last_block(e, lambda cp: cp.start())
        for e in range(N_EXPERTS):
            last_block(e, lambda cp: cp.wait())

    def row_copy(r, d):
        return pltpu.make_async_copy(h_ref.at[pl.ds(r, 1)], xs_ref.at[pl.ds(d, 1)], sem)

    def issue(r, carry):
        for kk in range(TOP_K):
            row_copy(r, dest_ref[base + r * TOP_K + kk]).start()
        return carry

    lax.fori_loop(0, tm, issue, 0)

    def drain(r, carry):
        for kk in range(TOP_K):
            row_copy(0, 0).wait()
        return carry

    lax.fori_loop(0, tm, drain, 0)


def _dispatch(dest_flat, pad_ends, h, n_rows, tm=256):
    T, W = h.shape
    return pl.pallas_call(
        _dispatch_body,
        grid_spec=pltpu.PrefetchScalarGridSpec(
            num_scalar_prefetch=2,
            grid=(T // tm,),
            in_specs=[pl.BlockSpec((tm, W), lambda i, dest, ends: (i, 0))],
            out_specs=pl.BlockSpec(memory_space=pl.ANY),
            scratch_shapes=[pltpu.VMEM((EXPERT_BLOCK, W), h.dtype),
                            pltpu.SemaphoreType.DMA(()), pltpu.SemaphoreType.DMA(())]),
        out_shape=jax.ShapeDtypeStruct((n_rows, W), h.dtype),
        compiler_params=_cparams(1),
        name="dispatch",
    )(dest_flat, pad_ends, h)


def _swiglu_deinterleaved(gu, fc):
    even = lax.broadcasted_iota(jnp.int32, (gu.shape[0], LANES), 1) % 2 == 0
    acts = []
    for c in range(fc // LANES):
        a = gu[:, c * LANES:(c + 1) * LANES]
        b = gu[:, fc + c * LANES:fc + (c + 1) * LANES]
        gate = jnp.where(even, a, pltpu.roll(b, 1, axis=1))
        up = jnp.where(even, pltpu.roll(a, LANES - 1, axis=1), b)
        gate = jnp.minimum(gate, SWIGLU_LIMIT)
        up = jnp.clip(up, -SWIGLU_LIMIT, SWIGLU_LIMIT)
        acts.append(((up + 1.0) * gate * jax.nn.sigmoid(SWIGLU_ALPHA * gate)).astype(BF16))
    return jnp.concatenate(acts, axis=1)


def _interleave_rows_bf16(a, b):
    return pltpu.bitcast(_pack_bf16_pair(a, b), BF16)


def _expert_body(ie_ref, ir_ref, in_ref, nu_ref,
                 xs_ref, wgu_ref, wdn_ref, bgu_ref, bdn_ref, ys_ref,
                 xlo, xhi, xraw, act, wgu_bf, wdn_bf, ystage, sem_x, sem_y, *, n_gu, n_dn):
    del ie_ref
    i = pl.program_id(0)
    j = pl.program_id(1)
    nrb = in_ref[i]
    row0 = ir_ref[i]
    fc = act.shape[2]
    nc = ystage.shape[2]
    half = xlo.shape[1]
    blk = EXPERT_BLOCK

    def rows(rb, first=0):
        return pl.ds(pl.multiple_of(first + rb * blk, blk), blk)

    @pl.when((i == 0) & (j == 0))
    def _():
        ystage[0] = jnp.zeros(ystage.shape[1:], F32)
        n_blocks = ys_ref.shape[0] // blk

        def tail(b, c):
            for cc in range(n_dn):
                cp = pltpu.make_async_copy(
                    ystage.at[0], ys_ref.at[rows(b), pl.ds(cc * nc, nc)], sem_y.at[0])
                cp.start()
                cp.wait()
            return c

        lax.fori_loop(nu_ref[0], n_blocks, tail, 0)

    @pl.when((nrb > 0) & (j == 0))
    def _():
        def x_copy(rb, slot):
            return pltpu.make_async_copy(xs_ref.at[rows(rb, row0)], xraw.at[slot], sem_x.at[slot])

        x_copy(0, 0).start()

        def load(rb, c):
            slot = rb % 2

            @pl.when(rb + 1 < nrb)
            def _():
                x_copy(rb + 1, 1 - slot).start()

            x_copy(rb, slot).wait()
            lo, hi = _unpack_bf16_pair(xraw[slot])
            xlo[rows(rb), :] = lo
            xhi[rows(rb), :] = hi
            return c

        lax.fori_loop(0, nrb, load, 0)

    @pl.when((nrb > 0) & (j < n_gu))
    def _():
        wgu_bf[...] = wgu_ref[...].astype(BF16)

        def block(rb, c):
            gu = (_dot(xlo[rows(rb), :], wgu_bf[:half, :]) + _dot(xhi[rows(rb), :], wgu_bf[half:, :])
                  + bgu_ref[...])
            act[j, rows(rb), :] = _swiglu_deinterleaved(gu, fc)
            return c

        lax.fori_loop(0, nrb, block, 0)

    @pl.when((nrb > 0) & (j >= n_gu))
    def _():
        jd = j - n_gu
        for jf in range(n_gu):
            for c in range(fc // LANES):
                a0 = jf * fc + c * (LANES // 2)
                b0 = a0 + fc // 2
                wdn_bf[jf * fc + c * LANES:jf * fc + (c + 1) * LANES, :] = _interleave_rows_bf16(
                    wdn_ref[a0:a0 + LANES // 2, :], wdn_ref[b0:b0 + LANES // 2, :])

        def y_copy(rb, slot):
            return pltpu.make_async_copy(
                ystage.at[slot], ys_ref.at[rows(rb, row0), pl.ds(pl.multiple_of(jd * nc, nc), nc)],
                sem_y.at[slot])

        def block(rb, c):
            slot = rb % 2

            @pl.when(rb >= 2)
            def _():
                y_copy(rb - 2, slot).wait()

            y = bdn_ref[...]
            for jf in range(n_gu):
                y = y + _dot(act[jf, rows(rb), :], wdn_bf[jf * fc:(jf + 1) * fc, :])
            ystage[slot] = y
            y_copy(rb, slot).start()
            return c

        lax.fori_loop(0, nrb, block, 0)

        @pl.when(nrb >= 2)
        def _():
            y_copy(nrb - 2, nrb % 2).wait()

        y_copy(nrb - 1, (nrb - 1) % 2).wait()


def _experts(items, xs, w_gate_up, w_down, b_gate_up, b_down, fc, nc):
    item_e, item_row0, item_nrb, n_used = items
    n_rows = xs.shape[0]
    E, D, F2 = w_gate_up.shape
    F = F2 // 2
    n_gu = F // fc
    n_dn = D // nc
    n_items = item_e.shape[0]

    def gu_chunk(i, j, ie, ir, inr, nu):
        return jnp.where(inr[i] > 0, jnp.minimum(j, n_gu - 1), n_gu - 1)

    def dn_chunk(i, j, ie, ir, inr, nu):
        return jnp.where(inr[i] > 0, jnp.maximum(j - n_gu, 0), n_dn - 1)

    return pl.pallas_call(
        functools.partial(_expert_body, n_gu=n_gu, n_dn=n_dn),
        grid_spec=pltpu.PrefetchScalarGridSpec(
            num_scalar_prefetch=4,
            grid=(n_items, n_gu + n_dn),
            in_specs=[
                pl.BlockSpec(memory_space=pl.ANY),
                pl.BlockSpec((None, D, 2 * fc), lambda i, j, ie, *a: (ie[i], 0, gu_chunk(i, j, ie, *a))),
                pl.BlockSpec((None, F, nc), lambda i, j, ie, *a: (ie[i], 0, dn_chunk(i, j, ie, *a))),
                pl.BlockSpec((None, 1, 2 * fc), lambda i, j, ie, *a: (ie[i], 0, gu_chunk(i, j, ie, *a))),
                pl.BlockSpec((None, 1, nc), lambda i, j, ie, *a: (ie[i], 0, dn_chunk(i, j, ie, *a))),
            ],
            out_specs=pl.BlockSpec(memory_space=pl.ANY),
            scratch_shapes=[
                pltpu.VMEM((EXPERT_ROWS, D // 2), BF16), pltpu.VMEM((EXPERT_ROWS, D // 2), BF16),
                pltpu.VMEM((2, EXPERT_BLOCK, D // 2), jnp.uint32),
                pltpu.VMEM((n_gu, EXPERT_ROWS, fc), BF16),
                pltpu.VMEM((D, 2 * fc), BF16), pltpu.VMEM((F, nc), BF16),
                pltpu.VMEM((2, EXPERT_BLOCK, nc), F32),
                pltpu.SemaphoreType.DMA((2,)), pltpu.SemaphoreType.DMA((2,))]),
        out_shape=jax.ShapeDtypeStruct((n_rows, D), F32),
        compiler_params=_cparams(2),
        name="experts",
    )(item_e, item_row0, item_nrb, n_used, xs, w_gate_up, w_down, b_gate_up[:, None, :],
      b_down[:, None, :])


def _combine_body(dest_ref, x1_ref, rg_ref, p_ref, ys_ref, gp_ref, wpg_ref, wpp_ref, gf_ref,
                  out_ref, buf, sem):
    tm = x1_ref.shape[0]
    base = pl.program_id(0) * (tm * TOP_K)

    def row_copy(r, kk, d):
        return pltpu.make_async_copy(ys_ref.at[pl.ds(d, 1)], buf.at[kk, pl.ds(r, 1)], sem)

    def issue(r, carry):
        for kk in range(TOP_K):
            row_copy(r, kk, dest_ref[base + r * TOP_K + kk]).start()
        return carry

    lax.fori_loop(0, tm, issue, 0)

    def drain(r, carry):
        for kk in range(TOP_K):
            row_copy(0, kk, 0).wait()
        return carry

    lax.fori_loop(0, tm, drain, 0)

    gates = rg_ref[...]
    x2 = x1_ref[...]
    for kk in range(TOP_K):
        x2 = x2 + gates[:, kk:kk + 1] * buf[kk]
    hp = _rms(x2, gp_ref[...], RMS_EPS).astype(BF16)
    gate = jax.nn.sigmoid(_dot(hp, wpg_ref[...]))
    proj = _dot(p_ref[...].astype(BF16), wpp_ref[...])
    x3 = x2 + gate * proj
    out_ref[...] = _rms(x3, gf_ref[...], RMS_EPS)


def _combine(dest_flat, x1, rg, p2d, ys, g_ple, w_gate, w_proj, g_final, tm=256):
    T, D = x1.shape
    P = p2d.shape[1]
    row = lambda i, dest: (i, 0)
    fixed = lambda i, dest: (0, 0)
    return pl.pallas_call(
        _combine_body,
        grid_spec=pltpu.PrefetchScalarGridSpec(
            num_scalar_prefetch=1,
            grid=(T // tm,),
            in_specs=[pl.BlockSpec((tm, D), row), pl.BlockSpec((tm, LANES), row),
                      pl.BlockSpec((tm, P), row), pl.BlockSpec(memory_space=pl.ANY),
                      pl.BlockSpec((1, D), fixed), pl.BlockSpec((D, D), fixed),
                      pl.BlockSpec((P, D), fixed), pl.BlockSpec((1, D), fixed)],
            out_specs=pl.BlockSpec((tm, D), row),
            scratch_shapes=[pltpu.VMEM((TOP_K, tm, D), F32), pltpu.SemaphoreType.DMA(())]),
        out_shape=jax.ShapeDtypeStruct((T, D), F32),
        compiler_params=_cparams(1),
        name="combine_ple",
    )(dest_flat, x1, rg, p2d, ys, g_ple.reshape(1, D), w_gate.astype(BF16),
      w_proj.astype(BF16), g_final.reshape(1, D))


def _routing_tables(ri, counts_block, n_items):
    counts = counts_block[0, :N_EXPERTS]
    padded = (counts + EXPERT_BLOCK - 1) // EXPERT_BLOCK * EXPERT_BLOCK
    pad_ends = jnp.cumsum(padded)
    pad_starts = pad_ends - padded
    idx = ri[:, :TOP_K]
    rank = ri[:, TOP_K:2 * TOP_K]
    dest = (pad_starts[idx] + rank).reshape(-1).astype(jnp.int32)

    per_expert = (padded + EXPERT_ROWS - 1) // EXPERT_ROWS
    item_ends = jnp.cumsum(per_expert)
    total = item_ends[-1]
    ids = jnp.arange(n_items, dtype=jnp.int32)
    e_of = jnp.sum(jnp.minimum(ids, total - 1)[:, None] >= item_ends[None, :], axis=1)
    e_of = jnp.clip(e_of, 0, N_EXPERTS - 1).astype(jnp.int32)
    local = ids - (item_ends - per_expert)[e_of]
    active = ids < total
    row0 = jnp.where(active, pad_starts[e_of] + local * EXPERT_ROWS, 0)
    left = jnp.clip(padded[e_of] - local * EXPERT_ROWS, 0, EXPERT_ROWS)
    nrb = jnp.where(active, left // EXPERT_BLOCK, 0)
    n_used = pad_ends[-1:] // EXPERT_BLOCK
    i32 = lambda a: a.astype(jnp.int32)
    return dest, i32(pad_ends), (e_of, i32(row0), i32(nrb), i32(n_used))


def _layer(x2d, p2d, B, S, norm_mix, w_in, lb, hgrn_out_norm, gmlp_ln_g, gmlp_ln_b, w_spatial,
           b_spatial, w_out, norm_ffn, w_router, b_router, w_gate_up, b_gate_up, w_down, b_down,
           norm_ple, w_ple_gate, w_ple_proj, g_out):
    T = x2d.shape[0]
    z = _inproj(x2d, norm_mix, w_in.astype(BF16))
    o = _hgrn(z, lb, hgrn_out_norm, B, S)
    sg = _gmlp(z, gmlp_ln_g, gmlp_ln_b, w_spatial, b_spatial)
    x1, h2, ri, rg, counts = _outproj(o, sg, x2d, w_out, norm_ffn, w_router, b_router)

    n_rows = T * TOP_K + N_EXPERTS * EXPERT_BLOCK
    n_items = N_EXPERTS + T * TOP_K // EXPERT_ROWS
    dest, pad_ends, items = _routing_tables(ri, counts, n_items)
    xs = _dispatch(dest, pad_ends, h2, n_rows)
    ys = _experts(items, xs, w_gate_up, w_down, b_gate_up, b_down, EXPERT_FC, EXPERT_NC)
    return _combine(dest, x1, rg, p2d, ys, norm_ple, w_ple_gate, w_ple_proj, g_out)


def kernel(x, p, norm_mix, w_in, lb_logits, hgrn_out_norm, gmlp_ln_g, gmlp_ln_b, w_spatial, b_spatial, w_out, norm_ffn, w_router, b_router, w_gate_up, b_gate_up, w_down, b_down, norm_ple, w_ple_gate, w_ple_proj, norm_final):
    B, S, D = x.shape
    depth = p.shape[0]
    assert depth == 1, "the final norm is fused into the last layer's combine kernel"
    lower_bounds = jnp.cumsum(jax.nn.softmax(lb_logits.astype(F32), axis=0), axis=0)
    out = _layer(x.reshape(B * S, D), p[0].reshape(B * S, -1), B, S, norm_mix[0], w_in[0],
                 lower_bounds[0], hgrn_out_norm[0], gmlp_ln_g[0], gmlp_ln_b[0], w_spatial[0],
                 b_spatial[0], w_out[0], norm_ffn[0], w_router[0], b_router[0], w_gate_up[0],
                 b_gate_up[0], w_down[0], b_down[0], norm_ple[0], w_ple_gate[0], w_ple_proj[0],
                 norm_final)
    return out.reshape(B, S, D)
```

```python
import functools
import math

import numpy as np
import jax
import jax.numpy as jnp
from jax import lax
from jax.experimental import pallas as pl
from jax.experimental.pallas import tpu as pltpu

F32 = jnp.float32
BF16 = jnp.bfloat16

HG_HEADS = 8
HG_D = 128
GM_GROUPS = 8
GM_CH = 128
GM_CHUNK = 128
N_EXPERTS = 32
TOP_K = 4
SWIGLU_LIMIT = 7.0
SWIGLU_ALPHA = 1.702
RMS_EPS = 1e-6
LN_EPS = 1e-5

LANES = 128
HGRN_TILE = 256
EXPERT_BLOCK = 256
EXPERT_ROWS = 3072
EXPERT_FC = 256
EXPERT_NC = 256
VMEM_LIMIT = 56 * 1024 * 1024


def _cparams(n_axes):
    return pltpu.CompilerParams(dimension_semantics=("arbitrary",) * n_axes,
                                vmem_limit_bytes=VMEM_LIMIT)


def _rms(x, g, eps):
    return x * lax.rsqrt(jnp.mean(x * x, axis=-1, keepdims=True) + eps) * g


def _dot(a, b):
    return jnp.dot(a, b, preferred_element_type=F32)


def _dot_nt(a, b):
    return lax.dot_general(a, b, (((1,), (1,)), ((), ())), preferred_element_type=F32)


def _bf16_bits(x):
    return pltpu.bitcast(x.astype(BF16).astype(F32), jnp.uint32) & jnp.uint32(0xFFFF0000)


def _pack_bf16_pair(lo, hi):
    return (_bf16_bits(lo) >> 16) | _bf16_bits(hi)


def _unpack_bf16_pair(w):
    lo = pltpu.bitcast(w << 16, F32).astype(BF16)
    hi = pltpu.bitcast(w & jnp.uint32(0xFFFF0000), F32).astype(BF16)
    return lo, hi


def _gelu(x):
    return 0.5 * x * (1.0 + lax.erf(x * (1.0 / math.sqrt(2.0))))


def _inproj_body(x_ref, g_ref, w_ref, z_ref, h_scr):
    @pl.when(pl.program_id(1) == 0)
    def _():
        h_scr[...] = _rms(x_ref[...], g_ref[...], RMS_EPS).astype(BF16)

    z_ref[...] = _dot(h_scr[...], w_ref[...])


def _inproj(x2d, g, w, tm=1024, tn=1024):
    T, D = x2d.shape
    N = w.shape[1]
    return pl.pallas_call(
        _inproj_body,
        grid=(T // tm, N // tn),
        in_specs=[pl.BlockSpec((tm, D), lambda i, j: (i, 0)),
                  pl.BlockSpec((1, D), lambda i, j: (0, 0)),
                  pl.BlockSpec((D, tn), lambda i, j: (0, j))],
        out_specs=pl.BlockSpec((tm, tn), lambda i, j: (i, j)),
        out_shape=jax.ShapeDtypeStruct((T, N), F32),
        scratch_shapes=[pltpu.VMEM((tm, D), BF16)],
        compiler_params=_cparams(2),
        name="inproj",
    )(x2d, g.reshape(1, D), w)


def _hgrn_constants(C):
    nlev = int(math.log2(C))
    r = np.arange(C)
    mats = [(r[None, :] <= r[:, None]), (r[None, :] > r[:, None])]
    level = np.full((C, C), -1, np.int32)
    level[r, r] = 0
    for l in range(1, nlev + 1):
        L = C >> l
        pos = r % (2 * L)
        mid = (r // (2 * L)) * (2 * L) + L - 1
        second = pos >= L
        m = np.where(second[:, None],
                     (r[None, :] > mid[:, None]) & (r[None, :] <= r[:, None]),
                     (r[None, :] > r[:, None]) & (r[None, :] <= mid[:, None]))
        mats.append(m)
        same = (r[:, None] // (2 * L)) == (r[None, :] // (2 * L))
        level[same & second[:, None] & (~second)[None, :]] = l
    w = np.concatenate(mats, axis=0).astype(np.float32)
    return jnp.asarray(w, BF16), jnp.asarray(level), nlev


def _hgrn_body(q_ref, f_ref, v_ref, og_ref, lb_ref, gn_ref, w_ref, lv_ref, o_ref, st_scr, *, nlev):
    C = q_ref.shape[0]

    @pl.when(pl.program_id(2) == 0)
    def _():
        st_scr[...] = jnp.zeros_like(st_scr)

    q = q_ref[...]
    q = q * jax.nn.sigmoid(q)
    lb = lb_ref[...]
    f = lb + (1.0 - lb) * jax.nn.sigmoid(f_ref[...])
    g = jnp.log(f)
    k = 1.0 - f
    v = v_ref[...]

    g1 = g.astype(BF16)
    r1 = g - g1.astype(F32)
    g2 = r1.astype(BF16)
    g3 = (r1 - g2.astype(F32)).astype(BF16)
    gcat = jnp.concatenate([g1, g2, g3], axis=1)

    def range_sum(i):
        e3 = _dot(w_ref[i * C:(i + 1) * C, :], gcat)
        return e3[:, :HG_D] + e3[:, HG_D:2 * HG_D] + e3[:, 2 * HG_D:]

    b = range_sum(0)
    suffix = range_sum(1)
    lv = lv_ref[...]

    a = jnp.where(lv == 0, _dot_nt(q.astype(BF16), k.astype(BF16)), 0.0)
    for l in range(1, nlev + 1):
        e = jnp.exp(range_sum(1 + l))
        a = jnp.where(lv == l, _dot_nt((q * e).astype(BF16), (k * e).astype(BF16)), a)

    st = st_scr[...]
    o = _dot_nt((q * jnp.exp(b)).astype(BF16), st.astype(BF16))
    o = o + _dot(a.astype(BF16), v.astype(BF16))

    khat = (k * jnp.exp(suffix)).astype(BF16)
    st_scr[...] = st * jnp.exp(b[C - 1:C, :]) + _dot(v.T.astype(BF16), khat)

    o = _rms(o, gn_ref[...], RMS_EPS)
    o_ref[...] = (o * jax.nn.sigmoid(og_ref[...])).astype(BF16)


def _hgrn(z, lb, gn, B, S):
    C = HGRN_TILE
    H = HG_HEADS
    n_s = S // C
    w, lv, nlev = _hgrn_constants(C)
    row = lambda b, h, s: b * n_s + s
    col = lambda off: pl.BlockSpec((C, HG_D), lambda b, h, s: (row(b, h, s), off * H + h))
    per_head = pl.BlockSpec((None, 1, HG_D), lambda b, h, s: (h, 0, 0))
    return pl.pallas_call(
        functools.partial(_hgrn_body, nlev=nlev),
        grid=(B, H, n_s),
        in_specs=[col(0), col(1), col(2), col(3), per_head, per_head,
                  pl.BlockSpec(w.shape, lambda b, h, s: (0, 0)),
                  pl.BlockSpec(lv.shape, lambda b, h, s: (0, 0))],
        out_specs=pl.BlockSpec((C, HG_D), lambda b, h, s: (row(b, h, s), h)),
        out_shape=jax.ShapeDtypeStruct((B * S, H * HG_D), BF16),
        scratch_shapes=[pltpu.VMEM((HG_D, HG_D), F32)],
        compiler_params=_cparams(3),
        name="hgrn",
    )(z, z, z, z, lb.reshape(H, 1, HG_D), gn.reshape(H, 1, HG_D), w, lv)


def _gmlp_body(u_ref, v_ref, lng_ref, lnb_ref, ws_ref, bias_ref, o_ref, w_scr):
    @pl.when(pl.program_id(0) == 0)
    def _():
        r = lax.broadcasted_iota(jnp.int32, ws_ref.shape, 1)
        c = lax.broadcasted_iota(jnp.int32, ws_ref.shape, 2)
        w_scr[...] = jnp.where(r >= c, ws_ref[...], 0.0).astype(BF16)

    v = _gelu(v_ref[...])
    mu = jnp.mean(v, axis=-1, keepdims=True)
    vc = v - mu
    vn = vc * lax.rsqrt(jnp.mean(vc * vc, axis=-1, keepdims=True) + LN_EPS)
    vn = (vn * lng_ref[...] + lnb_ref[...]).astype(BF16)
    n_chunks = u_ref.shape[0] // GM_CHUNK
    for c in range(n_chunks):
        rows = slice(c * GM_CHUNK, (c + 1) * GM_CHUNK)
        for g in range(GM_GROUPS):
            cols = slice(g * GM_CH, (g + 1) * GM_CH)
            mixed = _dot(w_scr[g], vn[rows, cols]) + bias_ref[:, cols]
            o_ref[rows, cols] = (_gelu(u_ref[rows, cols]) * mixed).astype(BF16)


def _gmlp(z, ln_g, ln_b, w_s, b_s, tg=512):
    T = z.shape[0]
    W = GM_GROUPS * GM_CH
    u_col = 4 * HG_HEADS * HG_D // W
    bias = jnp.repeat(b_s.T, GM_CH, axis=1)
    return pl.pallas_call(
        _gmlp_body,
        grid=(T // tg,),
        in_specs=[pl.BlockSpec((tg, W), lambda i: (i, u_col)),
                  pl.BlockSpec((tg, W), lambda i: (i, u_col + 1)),
                  pl.BlockSpec((1, W), lambda i: (0, 0)),
                  pl.BlockSpec((1, W), lambda i: (0, 0)),
                  pl.BlockSpec(w_s.shape, lambda i: (0, 0, 0)),
                  pl.BlockSpec(bias.shape, lambda i: (0, 0))],
        out_specs=pl.BlockSpec((tg, W), lambda i: (i, 0)),
        out_shape=jax.ShapeDtypeStruct((T, W), BF16),
        scratch_shapes=[pltpu.VMEM(w_s.shape, BF16)],
        compiler_params=_cparams(1),
        name="gmlp",
    )(z, z, ln_g.reshape(1, W), ln_b.reshape(1, W), w_s, bias)


def _outproj_body(o_ref, sg_ref, x_ref, wa_ref, wb_ref, g_ref, wr_ref, br_ref,
                  x1_ref, h_ref, ri_ref, rg_ref, cnt_ref, tri_scr, carry_scr):
    tm = x_ref.shape[0]

    @pl.when(pl.program_id(0) == 0)
    def _():
        r = lax.broadcasted_iota(jnp.int32, (tm, tm), 0)
        c = lax.broadcasted_iota(jnp.int32, (tm, tm), 1)
        tri_scr[...] = (c < r).astype(BF16)
        carry_scr[...] = jnp.zeros_like(carry_scr)

    x1 = x_ref[...] + _dot(o_ref[...], wa_ref[...]) + _dot(sg_ref[...], wb_ref[...])
    x1_ref[...] = x1
    h = _rms(x1, g_ref[...], RMS_EPS)
    half = h.shape[1] // 2
    h_ref[...] = _pack_bf16_pair(h[:, :half], h[:, half:])

    logits = jnp.dot(h, wr_ref[...], preferred_element_type=F32,
                     precision=lax.Precision.HIGHEST) + br_ref[...]
    lane = lax.broadcasted_iota(jnp.int32, (tm, LANES), 1)
    neg = jnp.float32(-jnp.inf)
    work = jnp.where(lane < N_EXPERTS, logits, neg)
    vals, idxs = [], []
    for _ in range(TOP_K):
        m = jnp.max(work, axis=-1, keepdims=True)
        i = jnp.min(jnp.where(work == m, lane, LANES), axis=-1, keepdims=True)
        vals.append(m)
        idxs.append(i)
        work = jnp.where(lane == i, neg, work)
    es = [jnp.exp(m - vals[0]) for m in vals]
    denom = es[0] + es[1] + es[2] + es[3]

    onehot = jnp.zeros((tm, LANES), F32)
    for i in idxs:
        onehot = onehot + (lane == i).astype(F32)
    before = _dot(tri_scr[...], onehot.astype(BF16)) + carry_scr[0:1, :]
    carry = carry_scr[0:1, :] + jnp.sum(onehot, axis=0, keepdims=True)
    carry_scr[...] = jnp.broadcast_to(carry, carry_scr.shape)
    cnt_ref[...] = jnp.broadcast_to(carry, cnt_ref.shape).astype(jnp.int32)

    ri = jnp.zeros((tm, LANES), jnp.int32)
    rg = jnp.zeros((tm, LANES), F32)
    for kk in range(TOP_K):
        rank = jnp.sum(jnp.where(lane == idxs[kk], before, 0.0), axis=-1, keepdims=True)
        ri = jnp.where(lane == kk, idxs[kk], ri)
        ri = jnp.where(lane == TOP_K + kk, rank.astype(jnp.int32), ri)
        rg = jnp.where(lane == kk, es[kk] / denom, rg)
    ri_ref[...] = ri
    rg_ref[...] = rg


def _outproj(o, sg, x2d, w_out, g, w_router, b_router, tm=512):
    T, D = x2d.shape
    Wh = o.shape[1]
    wa = w_out[:Wh].astype(BF16)
    wb = w_out[Wh:].astype(BF16)
    wr = jnp.zeros((D, LANES), F32).at[:, :N_EXPERTS].set(w_router)
    br = jnp.zeros((1, LANES), F32).at[0, :N_EXPERTS].set(b_router)
    row = lambda i: (i, 0)
    fixed = lambda i: (0, 0)
    return pl.pallas_call(
        _outproj_body,
        grid=(T // tm,),
        in_specs=[pl.BlockSpec((tm, Wh), row), pl.BlockSpec((tm, sg.shape[1]), row),
                  pl.BlockSpec((tm, D), row),
                  pl.BlockSpec(wa.shape, fixed), pl.BlockSpec(wb.shape, fixed),
                  pl.BlockSpec((1, D), fixed), pl.BlockSpec(wr.shape, fixed),
                  pl.BlockSpec(br.shape, fixed)],
        out_specs=[pl.BlockSpec((tm, D), row), pl.BlockSpec((tm, D // 2), row),
                   pl.BlockSpec((tm, LANES), row), pl.BlockSpec((tm, LANES), row),
                   pl.BlockSpec((8, LANES), fixed)],
        out_shape=[jax.ShapeDtypeStruct((T, D), F32), jax.ShapeDtypeStruct((T, D // 2), jnp.uint32),
                   jax.ShapeDtypeStruct((T, LANES), jnp.int32),
                   jax.ShapeDtypeStruct((T, LANES), F32),
                   jax.ShapeDtypeStruct((8, LANES), jnp.int32)],
        scratch_shapes=[pltpu.VMEM((tm, tm), BF16), pltpu.VMEM((8, LANES), F32)],
        compiler_params=_cparams(1),
        name="outproj_router",
    )(o, sg, x2d, wa, wb, g.reshape(1, D), wr, br)


def _dispatch_body(dest_ref, ends_ref, h_ref, xs_ref, zero_scr, sem, zsem):
    tm = h_ref.shape[0]
    base = pl.program_id(0) * (tm * TOP_K)

    @pl.when(pl.program_id(0) == 0)
    def _():
        zero_scr[...] = jnp.zeros_like(zero_scr)

        def last_block(e, op):
            end = ends_ref[e]
            prev = ends_ref[e - 1] if e else 0

            @pl.when(end > prev)
            def _():
                start = pl.multiple_of(end - EXPERT_BLOCK, EXPERT_BLOCK)
                op(pltpu.make_async_copy(zero_scr, xs_ref.at[pl.ds(start, EXPERT_BLOCK)], zsem))

        for e in range(N_EXPERTS):
            last_block(e, lambda cp: cp.start())
        for e in range(N_EXPERTS):
            last_block(e, lambda cp: cp.wait())

        def tail_copy(b):
            start = pl.multiple_of(b * EXPERT_BLOCK, EXPERT_BLOCK)
            return pltpu.make_async_copy(zero_scr, xs_ref.at[pl.ds(start, EXPERT_BLOCK)], zsem)

        first = ends_ref[N_EXPERTS - 1] // EXPERT_BLOCK
        n_blocks = xs_ref.shape[0] // EXPERT_BLOCK
        lax.fori_loop(first, n_blocks, lambda b, c: (tail_copy(b).start(), c)[1], 0)
        lax.fori_loop(first, n_blocks, lambda b, c: (tail_copy(b).wait(), c)[1], 0)

    def row_copy(r, d):
        return pltpu.make_async_copy(h_ref.at[pl.ds(r, 1)], xs_ref.at[pl.ds(d, 1)], sem)

    def issue(r, carry):
        for kk in range(TOP_K):
            row_copy(r, dest_ref[base + r * TOP_K + kk]).start()
        return carry

    lax.fori_loop(0, tm, issue, 0)

    def drain(r, carry):
        for kk in range(TOP_K):
            row_copy(0, 0).wait()
        return carry

    lax.fori_loop(0, tm, drain, 0)


def _dispatch(dest_flat, pad_ends, h, n_rows, tm=256):
    T, W = h.shape
    return pl.pallas_call(
        _dispatch_body,
        grid_spec=pltpu.PrefetchScalarGridSpec(
            num_scalar_prefetch=2,
            grid=(T // tm,),
            in_specs=[pl.BlockSpec((tm, W), lambda i, dest, ends: (i, 0))],
            out_specs=pl.BlockSpec(memory_space=pl.ANY),
            scratch_shapes=[pltpu.VMEM((EXPERT_BLOCK, W), h.dtype),
                            pltpu.SemaphoreType.DMA(()), pltpu.SemaphoreType.DMA(())]),
        out_shape=jax.ShapeDtypeStruct((n_rows, W), h.dtype),
        compiler_params=_cparams(1),
        name="dispatch",
    )(dest_flat, pad_ends, h)


def _swiglu_deinterleaved(gu, fc):
    even = lax.broadcasted_iota(jnp.int32, (gu.shape[0], LANES), 1) % 2 == 0
    acts = []
    for c in range(fc // LANES):
        a = gu[:, c * LANES:(c + 1) * LANES]
        b = gu[:, fc + c * LANES:fc + (c + 1) * LANES]
        gate = jnp.where(even, a, pltpu.roll(b, 1, axis=1))
        up = jnp.where(even, pltpu.roll(a, LANES - 1, axis=1), b)
        gate = jnp.minimum(gate, SWIGLU_LIMIT)
        up = jnp.clip(up, -SWIGLU_LIMIT, SWIGLU_LIMIT)
        acts.append(((up + 1.0) * gate * jax.nn.sigmoid(SWIGLU_ALPHA * gate)).astype(BF16))
    return jnp.concatenate(acts, axis=1)


def _interleave_rows_bf16(a, b):
    return pltpu.bitcast(_pack_bf16_pair(a, b), BF16)


def _expert_body(ie_ref, ir_ref, in_ref, nu_ref,
                 xs_ref, wgu_ref, wdn_ref, bgu_ref, bdn_ref, ys_ref,
                 xlo, xhi, xraw, act, wgu_bf, wdn_bf, ystage, sem_x, sem_y, *, n_gu, n_dn):
    del ie_ref
    i = pl.program_id(0)
    j = pl.program_id(1)
    nrb = in_ref[i]
    row0 = ir_ref[i]
    fc = act.shape[2]
    nc = ystage.shape[2]
    half = xlo.shape[1]
    blk = EXPERT_BLOCK

    def rows(rb, first=0):
        return pl.ds(pl.multiple_of(first + rb * blk, blk), blk)

    @pl.when((i == 0) & (j == 0))
    def _():
        ystage[0] = jnp.zeros(ystage.shape[1:], F32)
        n_blocks = ys_ref.shape[0] // blk

        def tail_copy(b, cc):
            return pltpu.make_async_copy(
                ystage.at[0], ys_ref.at[rows(b), pl.ds(cc * nc, nc)], sem_y.at[0])

        def tail_start(b, c):
            for cc in range(n_dn):
                tail_copy(b, cc).start()
            return c

        def tail_wait(b, c):
            for cc in range(n_dn):
                tail_copy(b, cc).wait()
            return c

        lax.fori_loop(nu_ref[0], n_blocks, tail_start, 0)
        lax.fori_loop(nu_ref[0], n_blocks, tail_wait, 0)

    @pl.when((nrb > 0) & (j == 0))
    def _():
        def x_copy(rb, slot):
            return pltpu.make_async_copy(xs_ref.at[rows(rb, row0)], xraw.at[slot], sem_x.at[slot])

        x_copy(0, 0).start()

        def load(rb, c):
            slot = rb % 2

            @pl.when(rb + 1 < nrb)
            def _():
                x_copy(rb + 1, 1 - slot).start()

            x_copy(rb, slot).wait()
            lo, hi = _unpack_bf16_pair(xraw[slot])
            xlo[rows(rb), :] = lo
            xhi[rows(rb), :] = hi
            return c

        lax.fori_loop(0, nrb, load, 0)

    @pl.when((nrb > 0) & (j < n_gu))
    def _():
        wgu_bf[...] = wgu_ref[...].astype(BF16)

        def block(rb, c):
            gu = (_dot(xlo[rows(rb), :], wgu_bf[:half, :]) + _dot(xhi[rows(rb), :], wgu_bf[half:, :])
                  + bgu_ref[...])
            act[j, rows(rb), :] = _swiglu_deinterleaved(gu, fc)
            return c

        lax.fori_loop(0, nrb, block, 0)

    @pl.when((nrb > 0) & (j >= n_gu))
    def _():
        jd = j - n_gu
        for jf in range(n_gu):
            for c in range(fc // LANES):
                a0 = jf * fc + c * (LANES // 2)
                b0 = a0 + fc // 2
                wdn_bf[jf * fc + c * LANES:jf * fc + (c + 1) * LANES, :] = _interleave_rows_bf16(
                    wdn_ref[a0:a0 + LANES // 2, :], wdn_ref[b0:b0 + LANES // 2, :])

        def y_copy(rb, slot):
            return pltpu.make_async_copy(
                ystage.at[slot], ys_ref.at[rows(rb, row0), pl.ds(pl.multiple_of(jd * nc, nc), nc)],
                sem_y.at[slot])

        def block(rb, c):
            slot = rb % 2

            @pl.when(rb >= 2)
            def _():
                y_copy(rb - 2, slot).wait()

            y = bdn_ref[...]
            for jf in range(n_gu):
                y = y + _dot(act[jf, rows(rb), :], wdn_bf[jf * fc:(jf + 1) * fc, :])
            ystage[slot] = y
            y_copy(rb, slot).start()
            return c

        lax.fori_loop(0, nrb, block, 0)

        @pl.when(nrb >= 2)
        def _():
            y_copy(nrb - 2, nrb % 2).wait()

        y_copy(nrb - 1, (nrb - 1) % 2).wait()


def _experts(items, xs, w_gate_up, w_down, b_gate_up, b_down, fc, nc):
    item_e, item_row0, item_nrb, n_used = items
    n_rows = xs.shape[0]
    E, D, F2 = w_gate_up.shape
    F = F2 // 2
    n_gu = F // fc
    n_dn = D // nc
    n_items = item_e.shape[0]

    def gu_chunk(i, j, ie, ir, inr, nu):
        return jnp.where(inr[i] > 0, jnp.minimum(j, n_gu - 1), n_gu - 1)

    def dn_chunk(i, j, ie, ir, inr, nu):
        return jnp.where(inr[i] > 0, jnp.maximum(j - n_gu, 0), n_dn - 1)

    return pl.pallas_call(
        functools.partial(_expert_body, n_gu=n_gu, n_dn=n_dn),
        grid_spec=pltpu.PrefetchScalarGridSpec(
            num_scalar_prefetch=4,
            grid=(n_items, n_gu + n_dn),
            in_specs=[
                pl.BlockSpec(memory_space=pl.ANY),
                pl.BlockSpec((None, D, 2 * fc), lambda i, j, ie, *a: (ie[i], 0, gu_chunk(i, j, ie, *a))),
                pl.BlockSpec((None, F, nc), lambda i, j, ie, *a: (ie[i], 0, dn_chunk(i, j, ie, *a))),
                pl.BlockSpec((None, 1, 2 * fc), lambda i, j, ie, *a: (ie[i], 0, gu_chunk(i, j, ie, *a))),
                pl.BlockSpec((None, 1, nc), lambda i, j, ie, *a: (ie[i], 0, dn_chunk(i, j, ie, *a))),
            ],
            out_specs=pl.BlockSpec(memory_space=pl.ANY),
            scratch_shapes=[
                pltpu.VMEM((EXPERT_ROWS, D // 2), BF16), pltpu.VMEM((EXPERT_ROWS, D // 2), BF16),
                pltpu.VMEM((2, EXPERT_BLOCK, D // 2), jnp.uint32),
                pltpu.VMEM((n_gu, EXPERT_ROWS, fc), BF16),
                pltpu.VMEM((D, 2 * fc), BF16), pltpu.VMEM((F, nc), BF16),
                pltpu.VMEM((2, EXPERT_BLOCK, nc), F32),
                pltpu.SemaphoreType.DMA((2,)), pltpu.SemaphoreType.DMA((2,))]),
        out_shape=jax.ShapeDtypeStruct((n_rows, D), F32),
        compiler_params=_cparams(2),
        name="experts",
    )(item_e, item_row0, item_nrb, n_used, xs, w_gate_up, w_down, b_gate_up[:, None, :],
      b_down[:, None, :])


def _combine_body(dest_ref, x1_ref, rg_ref, p_ref, ys_ref, gp_ref, wpg_ref, wpp_ref, gf_ref,
                  out_ref, buf, sem):
    tm = x1_ref.shape[0]
    base = pl.program_id(0) * (tm * TOP_K)

    def row_copy(r, kk, d):
        return pltpu.make_async_copy(ys_ref.at[pl.ds(d, 1)], buf.at[kk, pl.ds(r, 1)], sem)

    def issue(r, carry):
        for kk in range(TOP_K):
            row_copy(r, kk, dest_ref[base + r * TOP_K + kk]).start()
        return carry

    lax.fori_loop(0, tm, issue, 0)

    def drain(r, carry):
        for kk in range(TOP_K):
            row_copy(0, kk, 0).wait()
        return carry

    lax.fori_loop(0, tm, drain, 0)

    gates = rg_ref[...]
    x2 = x1_ref[...]
    for kk in range(TOP_K):
        x2 = x2 + gates[:, kk:kk + 1] * buf[kk]
    hp = _rms(x2, gp_ref[...], RMS_EPS).astype(BF16)
    gate = jax.nn.sigmoid(_dot(hp, wpg_ref[...]))
    proj = _dot(p_ref[...].astype(BF16), wpp_ref[...])
    x3 = x2 + gate * proj
    out_ref[...] = _rms(x3, gf_ref[...], RMS_EPS)


def _combine(dest_flat, x1, rg, p2d, ys, g_ple, w_gate, w_proj, g_final, tm=256):
    T, D = x1.shape
    P = p2d.shape[1]
    row = lambda i, dest: (i, 0)
    fixed = lambda i, dest: (0, 0)
    return pl.pallas_call(
        _combine_body,
        grid_spec=pltpu.PrefetchScalarGridSpec(
            num_scalar_prefetch=1,
            grid=(T // tm,),
            in_specs=[pl.BlockSpec((tm, D), row), pl.BlockSpec((tm, LANES), row),
                      pl.BlockSpec((tm, P), row), pl.BlockSpec(memory_space=pl.ANY),
                      pl.BlockSpec((1, D), fixed), pl.BlockSpec((D, D), fixed),
                      pl.BlockSpec((P, D), fixed), pl.BlockSpec((1, D), fixed)],
            out_specs=pl.BlockSpec((tm, D), row),
            scratch_shapes=[pltpu.VMEM((TOP_K, tm, D), F32), pltpu.SemaphoreType.DMA(())]),
        out_shape=jax.ShapeDtypeStruct((T, D), F32),
        compiler_params=_cparams(1),
        name="combine_ple",
    )(dest_flat, x1, rg, p2d, ys, g_ple.reshape(1, D), w_gate.astype(BF16),
      w_proj.astype(BF16), g_final.reshape(1, D))


def _routing_tables(ri, counts_block, n_items):
    counts = counts_block[0, :N_EXPERTS]
    padded = (counts + EXPERT_BLOCK - 1) // EXPERT_BLOCK * EXPERT_BLOCK
    pad_ends = jnp.cumsum(padded)
    pad_starts = pad_ends - padded
    idx = ri[:, :TOP_K]
    rank = ri[:, TOP_K:2 * TOP_K]
    dest = (pad_starts[idx] + rank).reshape(-1).astype(jnp.int32)

    per_expert = (padded + EXPERT_ROWS - 1) // EXPERT_ROWS
    item_ends = jnp.cumsum(per_expert)
    total = item_ends[-1]
    ids = jnp.arange(n_items, dtype=jnp.int32)
    e_of = jnp.sum(jnp.minimum(ids, total - 1)[:, None] >= item_ends[None, :], axis=1)
    e_of = jnp.clip(e_of, 0, N_EXPERTS - 1).astype(jnp.int32)
    local = ids - (item_ends - per_expert)[e_of]
    active = ids < total
    row0 = jnp.where(active, pad_starts[e_of] + local * EXPERT_ROWS, 0)
    left = jnp.clip(padded[e_of] - local * EXPERT_ROWS, 0, EXPERT_ROWS)
    nrb = jnp.where(active, left // EXPERT_BLOCK, 0)
    n_used = pad_ends[-1:] // EXPERT_BLOCK
    i32 = lambda a: a.astype(jnp.int32)
    return dest, i32(pad_ends), (e_of, i32(row0), i32(nrb), i32(n_used))


def _layer(x2d, p2d, B, S, norm_mix, w_in, lb, hgrn_out_norm, gmlp_ln_g, gmlp_ln_b, w_spatial,
           b_spatial, w_out, norm_ffn, w_router, b_router, w_gate_up, b_gate_up, w_down, b_down,
           norm_ple, w_ple_gate, w_ple_proj, g_out):
    T = x2d.shape[0]
    z = _inproj(x2d, norm_mix, w_in.astype(BF16))
    o = _hgrn(z, lb, hgrn_out_norm, B, S)
    sg = _gmlp(z, gmlp_ln_g, gmlp_ln_b, w_spatial, b_spatial)
    x1, h2, ri, rg, counts = _outproj(o, sg, x2d, w_out, norm_ffn, w_router, b_router)

    n_rows = T * TOP_K + N_EXPERTS * EXPERT_BLOCK
    n_items = N_EXPERTS + T * TOP_K // EXPERT_ROWS
    dest, pad_ends, items = _routing_tables(ri, counts, n_items)
    xs = _dispatch(dest, pad_ends, h2, n_rows)
    ys = _experts(items, xs, w_gate_up, w_down, b_gate_up, b_down, EXPERT_FC, EXPERT_NC)
    return _combine(dest, x1, rg, p2d, ys, norm_ple, w_ple_gate, w_ple_proj, g_out)


def kernel(x, p, norm_mix, w_in, lb_logits, hgrn_out_norm, gmlp_ln_g, gmlp_ln_b, w_spatial, b_spatial, w_out, norm_ffn, w_router, b_router, w_gate_up, b_gate_up, w_down, b_down, norm_ple, w_ple_gate, w_ple_proj, norm_final):
    B, S, D = x.shape
    depth = p.shape[0]
    assert depth == 1, "the final norm is fused into the last layer's combine kernel"
    lower_bounds = jnp.cumsum(jax.nn.softmax(lb_logits.astype(F32), axis=0), axis=0)
    out = _layer(x.reshape(B * S, D), p[0].reshape(B * S, -1), B, S, norm_mix[0], w_in[0],
                 lower_bounds[0], hgrn_out_norm[0], gmlp_ln_g[0], gmlp_ln_b[0], w_spatial[0],
                 b_spatial[0], w_out[0], norm_ffn[0], w_router[0], b_router[0], w_gate_up[0],
                 b_gate_up[0], w_down[0], b_down[0], norm_ple[0], w_ple_gate[0], w_ple_proj[0],
                 norm_final)
    return out.reshape(B, S, D)
```

```python
import functools
import math

import numpy as np
import jax
import jax.numpy as jnp
from jax import lax
from jax.experimental import pallas as pl
from jax.experimental.pallas import tpu as pltpu

F32 = jnp.float32
BF16 = jnp.bfloat16

HG_HEADS = 8
HG_D = 128
GM_GROUPS = 8
GM_CH = 128
GM_CHUNK = 128
N_EXPERTS = 32
TOP_K = 4
SWIGLU_LIMIT = 7.0
SWIGLU_ALPHA = 1.702
RMS_EPS = 1e-6
LN_EPS = 1e-5

LANES = 128
HGRN_TILE = 256
EXPERT_BLOCK = 256
EXPERT_ROWS = 2048
EXPERT_FC = 512
EXPERT_NC = 512
EXPERT_YSLOTS = 4
VMEM_LIMIT = 56 * 1024 * 1024


def _cparams(n_axes):
    return pltpu.CompilerParams(dimension_semantics=("arbitrary",) * n_axes,
                                vmem_limit_bytes=VMEM_LIMIT)


def _rms(x, g, eps):
    return x * lax.rsqrt(jnp.mean(x * x, axis=-1, keepdims=True) + eps) * g


def _dot(a, b):
    return jnp.dot(a, b, preferred_element_type=F32)


def _dot_nt(a, b):
    return lax.dot_general(a, b, (((1,), (1,)), ((), ())), preferred_element_type=F32)


def _bf16_bits(x):
    return pltpu.bitcast(x.astype(BF16).astype(F32), jnp.uint32) & jnp.uint32(0xFFFF0000)


def _pack_bf16_pair(lo, hi):
    return (_bf16_bits(lo) >> 16) | _bf16_bits(hi)


def _unpack_bf16_pair(w):
    lo = pltpu.bitcast(w << 16, F32).astype(BF16)
    hi = pltpu.bitcast(w & jnp.uint32(0xFFFF0000), F32).astype(BF16)
    return lo, hi


def _gelu(x):
    return 0.5 * x * (1.0 + lax.erf(x * (1.0 / math.sqrt(2.0))))


def _inproj_body(x_ref, g_ref, w_ref, z_ref, h_scr):
    @pl.when(pl.program_id(1) == 0)
    def _():
        h_scr[...] = _rms(x_ref[...], g_ref[...], RMS_EPS).astype(BF16)

    z_ref[...] = _dot(h_scr[...], w_ref[...])


def _inproj(x2d, g, w, tm=1024, tn=1024):
    T, D = x2d.shape
    N = w.shape[1]
    return pl.pallas_call(
        _inproj_body,
        grid=(T // tm, N // tn),
        in_specs=[pl.BlockSpec((tm, D), lambda i, j: (i, 0)),
                  pl.BlockSpec((1, D), lambda i, j: (0, 0)),
                  pl.BlockSpec((D, tn), lambda i, j: (0, j))],
        out_specs=pl.BlockSpec((tm, tn), lambda i, j: (i, j)),
        out_shape=jax.ShapeDtypeStruct((T, N), F32),
        scratch_shapes=[pltpu.VMEM((tm, D), BF16)],
        compiler_params=_cparams(2),
        name="inproj",
    )(x2d, g.reshape(1, D), w)


def _hgrn_constants(C):
    nlev = int(math.log2(C))
    r = np.arange(C)
    mats = [(r[None, :] <= r[:, None]), (r[None, :] > r[:, None])]
    level = np.full((C, C), -1, np.int32)
    level[r, r] = 0
    for l in range(1, nlev + 1):
        L = C >> l
        pos = r % (2 * L)
        mid = (r // (2 * L)) * (2 * L) + L - 1
        second = pos >= L
        m = np.where(second[:, None],
                     (r[None, :] > mid[:, None]) & (r[None, :] <= r[:, None]),
                     (r[None, :] > r[:, None]) & (r[None, :] <= mid[:, None]))
        mats.append(m)
        same = (r[:, None] // (2 * L)) == (r[None, :] // (2 * L))
        level[same & second[:, None] & (~second)[None, :]] = l
    w = np.concatenate(mats, axis=0).astype(np.float32)
    return jnp.asarray(w, BF16), jnp.asarray(level), nlev


def _hgrn_body(q_ref, f_ref, v_ref, og_ref, lb_ref, gn_ref, w_ref, lv_ref, o_ref, st_scr, *, nlev):
    C = q_ref.shape[0]

    @pl.when(pl.program_id(2) == 0)
    def _():
        st_scr[...] = jnp.zeros_like(st_scr)

    q = q_ref[...]
    q = q * jax.nn.sigmoid(q)
    lb = lb_ref[...]
    f = lb + (1.0 - lb) * jax.nn.sigmoid(f_ref[...])
    g = jnp.log(f)
    k = 1.0 - f
    v = v_ref[...]

    g1 = g.astype(BF16)
    r1 = g - g1.astype(F32)
    g2 = r1.astype(BF16)
    g3 = (r1 - g2.astype(F32)).astype(BF16)
    gcat = jnp.concatenate([g1, g2, g3], axis=1)

    def range_sum(i):
        e3 = _dot(w_ref[i * C:(i + 1) * C, :], gcat)
        return e3[:, :HG_D] + e3[:, HG_D:2 * HG_D] + e3[:, 2 * HG_D:]

    b = range_sum(0)
    suffix = range_sum(1)
    lv = lv_ref[...]

    a = jnp.where(lv == 0, _dot_nt(q.astype(BF16), k.astype(BF16)), 0.0)
    for l in range(1, nlev + 1):
        e = jnp.exp(range_sum(1 + l))
        a = jnp.where(lv == l, _dot_nt((q * e).astype(BF16), (k * e).astype(BF16)), a)

    st = st_scr[...]
    o = _dot_nt((q * jnp.exp(b)).astype(BF16), st.astype(BF16))
    o = o + _dot(a.astype(BF16), v.astype(BF16))

    khat = (k * jnp.exp(suffix)).astype(BF16)
    st_scr[...] = st * jnp.exp(b[C - 1:C, :]) + _dot(v.T.astype(BF16), khat)

    o = _rms(o, gn_ref[...], RMS_EPS)
    o_ref[...] = (o * jax.nn.sigmoid(og_ref[...])).astype(BF16)


def _hgrn(z, lb, gn, B, S):
    C = HGRN_TILE
    H = HG_HEADS
    n_s = S // C
    w, lv, nlev = _hgrn_constants(C)
    row = lambda b, h, s: b * n_s + s
    col = lambda off: pl.BlockSpec((C, HG_D), lambda b, h, s: (row(b, h, s), off * H + h))
    per_head = pl.BlockSpec((None, 1, HG_D), lambda b, h, s: (h, 0, 0))
    return pl.pallas_call(
        functools.partial(_hgrn_body, nlev=nlev),
        grid=(B, H, n_s),
        in_specs=[col(0), col(1), col(2), col(3), per_head, per_head,
                  pl.BlockSpec(w.shape, lambda b, h, s: (0, 0)),
                  pl.BlockSpec(lv.shape, lambda b, h, s: (0, 0))],
        out_specs=pl.BlockSpec((C, HG_D), lambda b, h, s: (row(b, h, s), h)),
        out_shape=jax.ShapeDtypeStruct((B * S, H * HG_D), BF16),
        scratch_shapes=[pltpu.VMEM((HG_D, HG_D), F32)],
        compiler_params=_cparams(3),
        name="hgrn",
    )(z, z, z, z, lb.reshape(H, 1, HG_D), gn.reshape(H, 1, HG_D), w, lv)


def _gmlp_body(u_ref, v_ref, lng_ref, lnb_ref, ws_ref, bias_ref, o_ref, w_scr):
    @pl.when(pl.program_id(0) == 0)
    def _():
        r = lax.broadcasted_iota(jnp.int32, ws_ref.shape, 1)
        c = lax.broadcasted_iota(jnp.int32, ws_ref.shape, 2)
        w_scr[...] = jnp.where(r >= c, ws_ref[...], 0.0).astype(BF16)

    v = _gelu(v_ref[...])
    mu = jnp.mean(v, axis=-1, keepdims=True)
    vc = v - mu
    vn = vc * lax.rsqrt(jnp.mean(vc * vc, axis=-1, keepdims=True) + LN_EPS)
    vn = (vn * lng_ref[...] + lnb_ref[...]).astype(BF16)
    n_chunks = u_ref.shape[0] // GM_CHUNK
    for c in range(n_chunks):
        rows = slice(c * GM_CHUNK, (c + 1) * GM_CHUNK)
        for g in range(GM_GROUPS):
            cols = slice(g * GM_CH, (g + 1) * GM_CH)
            mixed = _dot(w_scr[g], vn[rows, cols]) + bias_ref[:, cols]
            o_ref[rows, cols] = (_gelu(u_ref[rows, cols]) * mixed).astype(BF16)


def _gmlp(z, ln_g, ln_b, w_s, b_s, tg=512):
    T = z.shape[0]
    W = GM_GROUPS * GM_CH
    u_col = 4 * HG_HEADS * HG_D // W
    bias = jnp.repeat(b_s.T, GM_CH, axis=1)
    return pl.pallas_call(
        _gmlp_body,
        grid=(T // tg,),
        in_specs=[pl.BlockSpec((tg, W), lambda i: (i, u_col)),
                  pl.BlockSpec((tg, W), lambda i: (i, u_col + 1)),
                  pl.BlockSpec((1, W), lambda i: (0, 0)),
                  pl.BlockSpec((1, W), lambda i: (0, 0)),
                  pl.BlockSpec(w_s.shape, lambda i: (0, 0, 0)),
                  pl.BlockSpec(bias.shape, lambda i: (0, 0))],
        out_specs=pl.BlockSpec((tg, W), lambda i: (i, 0)),
        out_shape=jax.ShapeDtypeStruct((T, W), BF16),
        scratch_shapes=[pltpu.VMEM(w_s.shape, BF16)],
        compiler_params=_cparams(1),
        name="gmlp",
    )(z, z, ln_g.reshape(1, W), ln_b.reshape(1, W), w_s, bias)


def _outproj_body(o_ref, sg_ref, x_ref, wa_ref, wb_ref, g_ref, wr_ref, br_ref,
                  x1_ref, h_ref, ri_ref, rg_ref, cnt_ref, tri_scr, carry_scr):
    tm = x_ref.shape[0]

    @pl.when(pl.program_id(0) == 0)
    def _():
        r = lax.broadcasted_iota(jnp.int32, (tm, tm), 0)
        c = lax.broadcasted_iota(jnp.int32, (tm, tm), 1)
        tri_scr[...] = (c < r).astype(BF16)
        carry_scr[...] = jnp.zeros_like(carry_scr)

    x1 = x_ref[...] + _dot(o_ref[...], wa_ref[...]) + _dot(sg_ref[...], wb_ref[...])
    x1_ref[...] = x1
    h = _rms(x1, g_ref[...], RMS_EPS)
    half = h.shape[1] // 2
    h_ref[...] = _pack_bf16_pair(h[:, :half], h[:, half:])

    logits = jnp.dot(h, wr_ref[...], preferred_element_type=F32,
                     precision=lax.Precision.HIGHEST) + br_ref[...]
    lane = lax.broadcasted_iota(jnp.int32, (tm, LANES), 1)
    neg = jnp.float32(-jnp.inf)
    work = jnp.where(lane < N_EXPERTS, logits, neg)
    vals, idxs = [], []
    for _ in range(TOP_K):
        m = jnp.max(work, axis=-1, keepdims=True)
        i = jnp.min(jnp.where(work == m, lane, LANES), axis=-1, keepdims=True)
        vals.append(m)
        idxs.append(i)
        work = jnp.where(lane == i, neg, work)
    es = [jnp.exp(m - vals[0]) for m in vals]
    denom = es[0] + es[1] + es[2] + es[3]

    onehot = jnp.zeros((tm, LANES), F32)
    for i in idxs:
        onehot = onehot + (lane == i).astype(F32)
    before = _dot(tri_scr[...], onehot.astype(BF16)) + carry_scr[0:1, :]
    carry = carry_scr[0:1, :] + jnp.sum(onehot, axis=0, keepdims=True)
    carry_scr[...] = jnp.broadcast_to(carry, carry_scr.shape)
    cnt_ref[...] = jnp.broadcast_to(carry, cnt_ref.shape).astype(jnp.int32)

    ri = jnp.zeros((tm, LANES), jnp.int32)
    rg = jnp.zeros((tm, LANES), F32)
    for kk in range(TOP_K):
        rank = jnp.sum(jnp.where(lane == idxs[kk], before, 0.0), axis=-1, keepdims=True)
        ri = jnp.where(lane == kk, idxs[kk], ri)
        ri = jnp.where(lane == TOP_K + kk, rank.astype(jnp.int32), ri)
        rg = jnp.where(lane == kk, es[kk] / denom, rg)
    ri_ref[...] = ri
    rg_ref[...] = rg


def _outproj(o, sg, x2d, w_out, g, w_router, b_router, tm=512):
    T, D = x2d.shape
    Wh = o.shape[1]
    wa = w_out[:Wh].astype(BF16)
    wb = w_out[Wh:].astype(BF16)
    wr = jnp.zeros((D, LANES), F32).at[:, :N_EXPERTS].set(w_router)
    br = jnp.zeros((1, LANES), F32).at[0, :N_EXPERTS].set(b_router)
    row = lambda i: (i, 0)
    fixed = lambda i: (0, 0)
    return pl.pallas_call(
        _outproj_body,
        grid=(T // tm,),
        in_specs=[pl.BlockSpec((tm, Wh), row), pl.BlockSpec((tm, sg.shape[1]), row),
                  pl.BlockSpec((tm, D), row),
                  pl.BlockSpec(wa.shape, fixed), pl.BlockSpec(wb.shape, fixed),
                  pl.BlockSpec((1, D), fixed), pl.BlockSpec(wr.shape, fixed),
                  pl.BlockSpec(br.shape, fixed)],
        out_specs=[pl.BlockSpec((tm, D), row), pl.BlockSpec((tm, D // 2), row),
                   pl.BlockSpec((tm, LANES), row), pl.BlockSpec((tm, LANES), row),
                   pl.BlockSpec((8, LANES), fixed)],
        out_shape=[jax.ShapeDtypeStruct((T, D), F32), jax.ShapeDtypeStruct((T, D // 2), jnp.uint32),
                   jax.ShapeDtypeStruct((T, LANES), jnp.int32),
                   jax.ShapeDtypeStruct((T, LANES), F32),
                   jax.ShapeDtypeStruct((8, LANES), jnp.int32)],
        scratch_shapes=[pltpu.VMEM((tm, tm), BF16), pltpu.VMEM((8, LANES), F32)],
        compiler_params=_cparams(1),
        name="outproj_router",
    )(o, sg, x2d, wa, wb, g.reshape(1, D), wr, br)


def _dispatch_body(dest_ref, ends_ref, h_ref, xs_ref, zero_scr, sem, zsem):
    tm = h_ref.shape[0]
    base = pl.program_id(0) * (tm * TOP_K)

    @pl.when(pl.program_id(0) == 0)
    def _():
        zero_scr[...] = jnp.zeros_like(zero_scr)

        def last_block(e, op):
            end = ends_ref[e]
            prev = ends_ref[e - 1] if e else 0

            @pl.when(end > prev)
            def _():
                start = pl.multiple_of(end - EXPERT_BLOCK, EXPERT_BLOCK)
                op(pltpu.make_async_copy(zero_scr, xs_ref.at[pl.ds(start, EXPERT_BLOCK)], zsem))

        for e in range(N_EXPERTS):
            last_block(e, lambda cp: cp.start())
        for e in range(N_EXPERTS):
            last_block(e, lambda cp: cp.wait())

        def tail_copy(b):
            start = pl.multiple_of(b * EXPERT_BLOCK, EXPERT_BLOCK)
            return pltpu.make_async_copy(zero_scr, xs_ref.at[pl.ds(start, EXPERT_BLOCK)], zsem)

        first = ends_ref[N_EXPERTS - 1] // EXPERT_BLOCK
        n_blocks = xs_ref.shape[0] // EXPERT_BLOCK
        lax.fori_loop(first, n_blocks, lambda b, c: (tail_copy(b).start(), c)[1], 0)
        lax.fori_loop(first, n_blocks, lambda b, c: (tail_copy(b).wait(), c)[1], 0)

    def row_copy(r, d):
        return pltpu.make_async_copy(h_ref.at[pl.ds(r, 1)], xs_ref.at[pl.ds(d, 1)], sem)

    def issue(r, carry):
        for kk in range(TOP_K):
            row_copy(r, dest_ref[base + r * TOP_K + kk]).start()
        return carry

    lax.fori_loop(0, tm, issue, 0)

    def drain(r, carry):
        for kk in range(TOP_K):
            row_copy(0, 0).wait()
        return carry

    lax.fori_loop(0, tm, drain, 0)


def _dispatch(dest_flat, pad_ends, h, n_rows, tm=256):
    T, W = h.shape
    return pl.pallas_call(
        _dispatch_body,
        grid_spec=pltpu.PrefetchScalarGridSpec(
            num_scalar_prefetch=2,
            grid=(T // tm,),
            in_specs=[pl.BlockSpec((tm, W), lambda i, dest, ends: (i, 0))],
            out_specs=pl.BlockSpec(memory_space=pl.ANY),
            scratch_shapes=[pltpu.VMEM((EXPERT_BLOCK, W), h.dtype),
                            pltpu.SemaphoreType.DMA(()), pltpu.SemaphoreType.DMA(())]),
        out_shape=jax.ShapeDtypeStruct((n_rows, W), h.dtype),
        compiler_params=_cparams(1),
        name="dispatch",
    )(dest_flat, pad_ends, h)


def _swiglu_deinterleaved(gu, fc):
    even = lax.broadcasted_iota(jnp.int32, (gu.shape[0], LANES), 1) % 2 == 0
    acts = []
    for c in range(fc // LANES):
        a = gu[:, c * LANES:(c + 1) * LANES]
        b = gu[:, fc + c * LANES:fc + (c + 1) * LANES]
        gate = jnp.where(even, a, pltpu.roll(b, 1, axis=1))
        up = jnp.where(even, pltpu.roll(a, LANES - 1, axis=1), b)
        gate = jnp.minimum(gate, SWIGLU_LIMIT)
        up = jnp.clip(up, -SWIGLU_LIMIT, SWIGLU_LIMIT)
        acts.append(((up + 1.0) * gate * jax.nn.sigmoid(SWIGLU_ALPHA * gate)).astype(BF16))
    return jnp.concatenate(acts, axis=1)


def _interleave_rows_bf16(a, b):
    return pltpu.bitcast(_pack_bf16_pair(a, b), BF16)


def _expert_body(ie_ref, ir_ref, in_ref, nu_ref,
                 xs_ref, wgu_ref, wdn_ref, bgu_ref, bdn_ref, ys_ref,
                 xres, act, wgu_bf, wdn_bf, ystage, sem_x, sem_y, *, n_gu, n_dn):
    del ie_ref
    i = pl.program_id(0)
    j = pl.program_id(1)
    nrb = in_ref[i]
    row0 = ir_ref[i]
    fc = act.shape[2]
    nc = ystage.shape[2]
    blk = EXPERT_BLOCK

    def rows(rb, first=0):
        return pl.ds(pl.multiple_of(first + rb * blk, blk), blk)

    @pl.when((i == 0) & (j == 0))
    def _():
        ystage[0] = jnp.zeros(ystage.shape[1:], F32)
        n_blocks = ys_ref.shape[0] // blk

        def tail_copy(b, cc):
            return pltpu.make_async_copy(
                ystage.at[0], ys_ref.at[rows(b), pl.ds(cc * nc, nc)], sem_y.at[0])

        def tail_start(b, c):
            for cc in range(n_dn):
                tail_copy(b, cc).start()
            return c

        def tail_wait(b, c):
            for cc in range(n_dn):
                tail_copy(b, cc).wait()
            return c

        lax.fori_loop(nu_ref[0], n_blocks, tail_start, 0)
        lax.fori_loop(nu_ref[0], n_blocks, tail_wait, 0)

    def x_copy(rb):
        return pltpu.make_async_copy(xs_ref.at[rows(rb, row0)], xres.at[rows(rb)], sem_x)

    def for_blocks(fn):
        lax.fori_loop(0, nrb, lambda rb, c: (fn(rb), c)[1], 0)

    @pl.when((nrb > 0) & (j < n_gu))
    def _():
        @pl.when(j == 0)
        def _():
            for_blocks(lambda rb: x_copy(rb).start())

        wgu_bf[...] = wgu_ref[...].astype(BF16)

        @pl.when(j == 0)
        def _():
            for_blocks(lambda rb: x_copy(rb).wait())

        def block(rb):
            x = jnp.concatenate(_unpack_bf16_pair(xres[rows(rb), :]), axis=1)
            gu = _dot(x, wgu_bf[...]) + bgu_ref[...]
            act[j, rows(rb), :] = _swiglu_deinterleaved(gu, fc)

        for_blocks(block)

    @pl.when((nrb > 0) & (j >= n_gu))
    def _():
        jd = j - n_gu
        n_slots = ystage.shape[0]
        for jf in range(n_gu):
            for c in range(fc // LANES):
                a0 = jf * fc + c * (LANES // 2)
                b0 = a0 + fc // 2
                wdn_bf[jf * fc + c * LANES:jf * fc + (c + 1) * LANES, :] = _interleave_rows_bf16(
                    wdn_ref[a0:a0 + LANES // 2, :], wdn_ref[b0:b0 + LANES // 2, :])

        def y_copy(rb):
            slot = rb % n_slots
            return pltpu.make_async_copy(
                ystage.at[slot], ys_ref.at[rows(rb, row0), pl.ds(pl.multiple_of(jd * nc, nc), nc)],
                sem_y.at[slot])

        def block(rb):
            @pl.when(rb >= n_slots)
            def _():
                y_copy(rb - n_slots).wait()

            a = jnp.concatenate([act[jf, rows(rb), :] for jf in range(n_gu)], axis=1)
            ystage[rb % n_slots] = _dot(a, wdn_bf[...]) + bdn_ref[...]
            y_copy(rb).start()

        for_blocks(block)
        for back in range(1, n_slots + 1):
            @pl.when(nrb >= back)
            def _():
                y_copy(nrb - back).wait()


def _experts(items, xs, w_gate_up, w_down, b_gate_up, b_down, fc, nc):
    item_e, item_row0, item_nrb, n_used = items
    n_rows = xs.shape[0]
    E, D, F2 = w_gate_up.shape
    F = F2 // 2
    n_gu = F // fc
    n_dn = D // nc
    n_items = item_e.shape[0]

    def gu_chunk(i, j, ie, ir, inr, nu):
        return jnp.where(inr[i] > 0, jnp.minimum(j, n_gu - 1), n_gu - 1)

    def dn_chunk(i, j, ie, ir, inr, nu):
        return jnp.where(inr[i] > 0, jnp.maximum(j - n_gu, 0), n_dn - 1)

    return pl.pallas_call(
        functools.partial(_expert_body, n_gu=n_gu, n_dn=n_dn),
        grid_spec=pltpu.PrefetchScalarGridSpec(
            num_scalar_prefetch=4,
            grid=(n_items, n_gu + n_dn),
            in_specs=[
                pl.BlockSpec(memory_space=pl.ANY),
                pl.BlockSpec((None, D, 2 * fc), lambda i, j, ie, *a: (ie[i], 0, gu_chunk(i, j, ie, *a))),
                pl.BlockSpec((None, F, nc), lambda i, j, ie, *a: (ie[i], 0, dn_chunk(i, j, ie, *a))),
                pl.BlockSpec((None, 1, 2 * fc), lambda i, j, ie, *a: (ie[i], 0, gu_chunk(i, j, ie, *a))),
                pl.BlockSpec((None, 1, nc), lambda i, j, ie, *a: (ie[i], 0, dn_chunk(i, j, ie, *a))),
            ],
            out_specs=pl.BlockSpec(memory_space=pl.ANY),
            scratch_shapes=[
                pltpu.VMEM((EXPERT_ROWS, D // 2), jnp.uint32),
                pltpu.VMEM((n_gu, EXPERT_ROWS, fc), BF16),
                pltpu.VMEM((D, 2 * fc), BF16), pltpu.VMEM((F, nc), BF16),
                pltpu.VMEM((EXPERT_YSLOTS, EXPERT_BLOCK, nc), F32),
                pltpu.SemaphoreType.DMA(()), pltpu.SemaphoreType.DMA((EXPERT_YSLOTS,))]),
        out_shape=jax.ShapeDtypeStruct((n_rows, D), F32),
        compiler_params=_cparams(2),
        name="experts",
    )(item_e, item_row0, item_nrb, n_used, xs, w_gate_up, w_down, b_gate_up[:, None, :],
      b_down[:, None, :])


def _combine_body(dest_ref, x1_ref, rg_ref, p_ref, ys_ref, gp_ref, wpg_ref, wpp_ref, gf_ref,
                  out_ref, buf, sem):
    tm = x1_ref.shape[0]
    base = pl.program_id(0) * (tm * TOP_K)

    def row_copy(r, kk, d):
        return pltpu.make_async_copy(ys_ref.at[pl.ds(d, 1)], buf.at[kk, pl.ds(r, 1)], sem)

    def issue(r, carry):
        for kk in range(TOP_K):
            row_copy(r, kk, dest_ref[base + r * TOP_K + kk]).start()
        return carry

    lax.fori_loop(0, tm, issue, 0)

    def drain(r, carry):
        for kk in range(TOP_K):
            row_copy(0, kk, 0).wait()
        return carry

    lax.fori_loop(0, tm, drain, 0)

    gates = rg_ref[...]
    x2 = x1_ref[...]
    for kk in range(TOP_K):
        x2 = x2 + gates[:, kk:kk + 1] * buf[kk]
    hp = _rms(x2, gp_ref[...], RMS_EPS).astype(BF16)
    gate = jax.nn.sigmoid(_dot(hp, wpg_ref[...]))
    proj = _dot(p_ref[...].astype(BF16), wpp_ref[...])
    x3 = x2 + gate * proj
    out_ref[...] = _rms(x3, gf_ref[...], RMS_EPS)


def _combine(dest_flat, x1, rg, p2d, ys, g_ple, w_gate, w_proj, g_final, tm=256):
    T, D = x1.shape
    P = p2d.shape[1]
    row = lambda i, dest: (i, 0)
    fixed = lambda i, dest: (0, 0)
    return pl.pallas_call(
        _combine_body,
        grid_spec=pltpu.PrefetchScalarGridSpec(
            num_scalar_prefetch=1,
            grid=(T // tm,),
            in_specs=[pl.BlockSpec((tm, D), row), pl.BlockSpec((tm, LANES), row),
                      pl.BlockSpec((tm, P), row), pl.BlockSpec(memory_space=pl.ANY),
                      pl.BlockSpec((1, D), fixed), pl.BlockSpec((D, D), fixed),
                      pl.BlockSpec((P, D), fixed), pl.BlockSpec((1, D), fixed)],
            out_specs=pl.BlockSpec((tm, D), row),
            scratch_shapes=[pltpu.VMEM((TOP_K, tm, D), F32), pltpu.SemaphoreType.DMA(())]),
        out_shape=jax.ShapeDtypeStruct((T, D), F32),
        compiler_params=_cparams(1),
        name="combine_ple",
    )(dest_flat, x1, rg, p2d, ys, g_ple.reshape(1, D), w_gate.astype(BF16),
      w_proj.astype(BF16), g_final.reshape(1, D))


def _routing_tables(ri, counts_block, n_items):
    counts = counts_block[0, :N_EXPERTS]
    padded = (counts + EXPERT_BLOCK - 1) // EXPERT_BLOCK * EXPERT_BLOCK
    pad_ends = jnp.cumsum(padded)
    pad_starts = pad_ends - padded
    idx = ri[:, :TOP_K]
    rank = ri[:, TOP_K:2 * TOP_K]
    dest = (pad_starts[idx] + rank).reshape(-1).astype(jnp.int32)

    per_expert = (padded + EXPERT_ROWS - 1) // EXPERT_ROWS
    item_ends = jnp.cumsum(per_expert)
    total = item_ends[-1]
    ids = jnp.arange(n_items, dtype=jnp.int32)
    e_of = jnp.sum(jnp.minimum(ids, total - 1)[:, None] >= item_ends[None, :], axis=1)
    e_of = jnp.clip(e_of, 0, N_EXPERTS - 1).astype(jnp.int32)
    local = ids - (item_ends - per_expert)[e_of]
    active = ids < total
    row0 = jnp.where(active, pad_starts[e_of] + local * EXPERT_ROWS, 0)
    left = jnp.clip(padded[e_of] - local * EXPERT_ROWS, 0, EXPERT_ROWS)
    nrb = jnp.where(active, left // EXPERT_BLOCK, 0)
    n_used = pad_ends[-1:] // EXPERT_BLOCK
    i32 = lambda a: a.astype(jnp.int32)
    return dest, i32(pad_ends), (e_of, i32(row0), i32(nrb), i32(n_used))


def _layer(x2d, p2d, B, S, norm_mix, w_in, lb, hgrn_out_norm, gmlp_ln_g, gmlp_ln_b, w_spatial,
           b_spatial, w_out, norm_ffn, w_router, b_router, w_gate_up, b_gate_up, w_down, b_down,
           norm_ple, w_ple_gate, w_ple_proj, g_out):
    T = x2d.shape[0]
    z = _inproj(x2d, norm_mix, w_in.astype(BF16))
    o = _hgrn(z, lb, hgrn_out_norm, B, S)
    sg = _gmlp(z, gmlp_ln_g, gmlp_ln_b, w_spatial, b_spatial)
    x1, h2, ri, rg, counts = _outproj(o, sg, x2d, w_out, norm_ffn, w_router, b_router)

    n_rows = T * TOP_K + N_EXPERTS * EXPERT_BLOCK
    n_items = N_EXPERTS + T * TOP_K // EXPERT_ROWS
    dest, pad_ends, items = _routing_tables(ri, counts, n_items)
    xs = _dispatch(dest, pad_ends, h2, n_rows)
    ys = _experts(items, xs, w_gate_up, w_down, b_gate_up, b_down, EXPERT_FC, EXPERT_NC)
    return _combine(dest, x1, rg, p2d, ys, norm_ple, w_ple_gate, w_ple_proj, g_out)


def kernel(x, p, norm_mix, w_in, lb_logits, hgrn_out_norm, gmlp_ln_g, gmlp_ln_b, w_spatial, b_spatial, w_out, norm_ffn, w_router, b_router, w_gate_up, b_gate_up, w_down, b_down, norm_ple, w_ple_gate, w_ple_proj, norm_final):
    B, S, D = x.shape
    depth = p.shape[0]
    assert depth == 1, "the final norm is fused into the last layer's combine kernel"
    lower_bounds = jnp.cumsum(jax.nn.softmax(lb_logits.astype(F32), axis=0), axis=0)
    out = _layer(x.reshape(B * S, D), p[0].reshape(B * S, -1), B, S, norm_mix[0], w_in[0],
                 lower_bounds[0], hgrn_out_norm[0], gmlp_ln_g[0], gmlp_ln_b[0], w_spatial[0],
                 b_spatial[0], w_out[0], norm_ffn[0], w_router[0], b_router[0], w_gate_up[0],
                 b_gate_up[0], w_down[0], b_down[0], norm_ple[0], w_ple_gate[0], w_ple_proj[0],
                 norm_final)
    return out.reshape(B, S, D)
```

```python
import functools
import math

import numpy as np
import jax
import jax.numpy as jnp
from jax import lax
from jax.experimental import pallas as pl
from jax.experimental.pallas import tpu as pltpu

F32 = jnp.float32
BF16 = jnp.bfloat16

HG_HEADS = 8
HG_D = 128
GM_GROUPS = 8
GM_CH = 128
GM_CHUNK = 128
N_EXPERTS = 32
TOP_K = 4
SWIGLU_LIMIT = 7.0
SWIGLU_ALPHA = 1.702
RMS_EPS = 1e-6
LN_EPS = 1e-5

LANES = 128
HGRN_TILE = 256
EXPERT_BLOCK = 256
EXPERT_ROWS = 2048
EXPERT_FC = 512
EXPERT_NC = 512
EXPERT_YSLOTS = 4
VMEM_LIMIT = 56 * 1024 * 1024


def _cparams(n_axes):
    return pltpu.CompilerParams(dimension_semantics=("arbitrary",) * n_axes,
                                vmem_limit_bytes=VMEM_LIMIT)


def _rms(x, g, eps):
    return x * lax.rsqrt(jnp.mean(x * x, axis=-1, keepdims=True) + eps) * g


def _dot(a, b):
    return jnp.dot(a, b, preferred_element_type=F32)


def _dot_nt(a, b):
    return lax.dot_general(a, b, (((1,), (1,)), ((), ())), preferred_element_type=F32)


def _bf16_bits(x):
    return pltpu.bitcast(x.astype(BF16).astype(F32), jnp.uint32) & jnp.uint32(0xFFFF0000)


def _pack_bf16_pair(lo, hi):
    return (_bf16_bits(lo) >> 16) | _bf16_bits(hi)


def _unpack_bf16_pair(w):
    lo = pltpu.bitcast(w << 16, F32).astype(BF16)
    hi = pltpu.bitcast(w & jnp.uint32(0xFFFF0000), F32).astype(BF16)
    return lo, hi


def _gelu(x):
    return 0.5 * x * (1.0 + lax.erf(x * (1.0 / math.sqrt(2.0))))


def _inproj_body(x_ref, g_ref, w_ref, z_ref, h_scr):
    @pl.when(pl.program_id(1) == 0)
    def _():
        h_scr[...] = _rms(x_ref[...], g_ref[...], RMS_EPS).astype(BF16)

    z_ref[...] = _dot(h_scr[...], w_ref[...])


def _inproj(x2d, g, w, tm=1024, tn=1024):
    T, D = x2d.shape
    N = w.shape[1]
    return pl.pallas_call(
        _inproj_body,
        grid=(T // tm, N // tn),
        in_specs=[pl.BlockSpec((tm, D), lambda i, j: (i, 0)),
                  pl.BlockSpec((1, D), lambda i, j: (0, 0)),
                  pl.BlockSpec((D, tn), lambda i, j: (0, j))],
        out_specs=pl.BlockSpec((tm, tn), lambda i, j: (i, j)),
        out_shape=jax.ShapeDtypeStruct((T, N), F32),
        scratch_shapes=[pltpu.VMEM((tm, D), BF16)],
        compiler_params=_cparams(2),
        name="inproj",
    )(x2d, g.reshape(1, D), w)


HGRN_MIN_BROADCAST = 8


def _hgrn_constants(C):
    nlev = int(math.log2(C))
    r = np.arange(C)
    mats = [(r[None, :] <= r[:, None])]
    level = np.full((C, C), -1, np.int32)
    level[r, r] = 0
    for l in range(1, nlev + 1):
        L = C >> l
        pos = r % (2 * L)
        mid = (r // (2 * L)) * (2 * L) + L - 1
        second = pos >= L
        if L < HGRN_MIN_BROADCAST:
            mats.append(np.where(second[:, None],
                                 (r[None, :] > mid[:, None]) & (r[None, :] <= r[:, None]),
                                 (r[None, :] > r[:, None]) & (r[None, :] <= mid[:, None])))
        same = (r[:, None] // (2 * L)) == (r[None, :] // (2 * L))
        level[same & second[:, None] & (~second)[None, :]] = l
    h = C // 2
    assert np.array_equal(level[:h, :h], level[h:, h:])
    w = np.concatenate(mats, axis=0).astype(np.float32)
    return jnp.asarray(w, BF16), jnp.asarray(level[:h, :h]), nlev


def _hgrn_body(q_ref, f_ref, v_ref, og_ref, lb_ref, gn_ref, w_ref, lv_ref, o_ref, st_scr, *, nlev):
    C = q_ref.shape[0]
    h = C // 2

    @pl.when(pl.program_id(2) == 0)
    def _():
        st_scr[...] = jnp.zeros_like(st_scr)

    q = q_ref[...]
    q = q * jax.nn.sigmoid(q)
    lb = lb_ref[...]
    f = lb + (1.0 - lb) * jax.nn.sigmoid(f_ref[...])
    g = jnp.log(f)
    k = 1.0 - f
    vb = v_ref[...].astype(BF16)

    g1 = g.astype(BF16)
    r1 = g - g1.astype(F32)
    g2 = r1.astype(BF16)
    g3 = (r1 - g2.astype(F32)).astype(BF16)
    gcat = jnp.concatenate([g1, g2, g3], axis=1)

    def range_sum(i):
        e3 = _dot(w_ref[i * C:(i + 1) * C, :], gcat)
        return e3[:, :HG_D] + e3[:, HG_D:2 * HG_D] + e3[:, 2 * HG_D:]

    b = range_sum(0)
    b_last = b[C - 1:C, :]

    def level_exponent(l):
        L = C >> l
        if L < HGRN_MIN_BROADCAST:
            return range_sum(1 + (l - (nlev + 1 - int(math.log2(HGRN_MIN_BROADCAST)))))
        mids = [jnp.broadcast_to(b[s + L - 1:s + L, :], (2 * L, HG_D)) for s in range(0, C, 2 * L)]
        mid = mids[0] if len(mids) == 1 else jnp.concatenate(mids, axis=0)
        return -jnp.abs(b - mid)

    def scaled(l):
        e = jnp.exp(level_exponent(l))
        return (q * e).astype(BF16), (k * e).astype(BF16)

    ql, kl = scaled(1)
    a_cross = _dot_nt(ql[h:], kl[:h])
    lv = lv_ref[...]
    qb, kb = q.astype(BF16), k.astype(BF16)
    a_top = jnp.where(lv == 0, _dot_nt(qb[:h], kb[:h]), 0.0)
    a_bot = jnp.where(lv == 0, _dot_nt(qb[h:], kb[h:]), 0.0)
    for l in range(2, nlev + 1):
        ql, kl = scaled(l)
        a_top = jnp.where(lv == l, _dot_nt(ql[:h], kl[:h]), a_top)
        a_bot = jnp.where(lv == l, _dot_nt(ql[h:], kl[h:]), a_bot)

    st = st_scr[...]
    o_top = _dot(a_top.astype(BF16), vb[:h])
    o_bot = _dot(a_cross.astype(BF16), vb[:h]) + _dot(a_bot.astype(BF16), vb[h:])
    o = _dot_nt((q * jnp.exp(b)).astype(BF16), st.astype(BF16))
    o = o + jnp.concatenate([o_top, o_bot], axis=0)

    khat = (k * jnp.exp(b_last - b)).astype(BF16)
    st_scr[...] = st * jnp.exp(b_last) + _dot(v_ref[...].T.astype(BF16), khat)

    o = _rms(o, gn_ref[...], RMS_EPS)
    o_ref[...] = (o * jax.nn.sigmoid(og_ref[...])).astype(BF16)


def _hgrn(z, lb, gn, B, S):
    C = HGRN_TILE
    H = HG_HEADS
    n_s = S // C
    w, lv, nlev = _hgrn_constants(C)
    row = lambda b, h, s: b * n_s + s
    col = lambda off: pl.BlockSpec((C, HG_D), lambda b, h, s: (row(b, h, s), off * H + h))
    per_head = pl.BlockSpec((None, 1, HG_D), lambda b, h, s: (h, 0, 0))
    return pl.pallas_call(
        functools.partial(_hgrn_body, nlev=nlev),
        grid=(B, H, n_s),
        in_specs=[col(0), col(1), col(2), col(3), per_head, per_head,
                  pl.BlockSpec(w.shape, lambda b, h, s: (0, 0)),
                  pl.BlockSpec(lv.shape, lambda b, h, s: (0, 0))],
        out_specs=pl.BlockSpec((C, HG_D), lambda b, h, s: (row(b, h, s), h)),
        out_shape=jax.ShapeDtypeStruct((B * S, H * HG_D), BF16),
        scratch_shapes=[pltpu.VMEM((HG_D, HG_D), F32)],
        compiler_params=_cparams(3),
        name="hgrn",
    )(z, z, z, z, lb.reshape(H, 1, HG_D), gn.reshape(H, 1, HG_D), w, lv)


def _gmlp_body(u_ref, v_ref, lng_ref, lnb_ref, ws_ref, bias_ref, o_ref, w_scr):
    @pl.when(pl.program_id(0) == 0)
    def _():
        r = lax.broadcasted_iota(jnp.int32, ws_ref.shape, 1)
        c = lax.broadcasted_iota(jnp.int32, ws_ref.shape, 2)
        w_scr[...] = jnp.where(r >= c, ws_ref[...], 0.0).astype(BF16)

    v = _gelu(v_ref[...])
    mu = jnp.mean(v, axis=-1, keepdims=True)
    vc = v - mu
    vn = vc * lax.rsqrt(jnp.mean(vc * vc, axis=-1, keepdims=True) + LN_EPS)
    vn = (vn * lng_ref[...] + lnb_ref[...]).astype(BF16)
    n_chunks = u_ref.shape[0] // GM_CHUNK
    for c in range(n_chunks):
        rows = slice(c * GM_CHUNK, (c + 1) * GM_CHUNK)
        for g in range(GM_GROUPS):
            cols = slice(g * GM_CH, (g + 1) * GM_CH)
            mixed = _dot(w_scr[g], vn[rows, cols]) + bias_ref[:, cols]
            o_ref[rows, cols] = (_gelu(u_ref[rows, cols]) * mixed).astype(BF16)


def _gmlp(z, ln_g, ln_b, w_s, b_s, tg=512):
    T = z.shape[0]
    W = GM_GROUPS * GM_CH
    u_col = 4 * HG_HEADS * HG_D // W
    bias = jnp.repeat(b_s.T, GM_CH, axis=1)
    return pl.pallas_call(
        _gmlp_body,
        grid=(T // tg,),
        in_specs=[pl.BlockSpec((tg, W), lambda i: (i, u_col)),
                  pl.BlockSpec((tg, W), lambda i: (i, u_col + 1)),
                  pl.BlockSpec((1, W), lambda i: (0, 0)),
                  pl.BlockSpec((1, W), lambda i: (0, 0)),
                  pl.BlockSpec(w_s.shape, lambda i: (0, 0, 0)),
                  pl.BlockSpec(bias.shape, lambda i: (0, 0))],
        out_specs=pl.BlockSpec((tg, W), lambda i: (i, 0)),
        out_shape=jax.ShapeDtypeStruct((T, W), BF16),
        scratch_shapes=[pltpu.VMEM(w_s.shape, BF16)],
        compiler_params=_cparams(1),
        name="gmlp",
    )(z, z, ln_g.reshape(1, W), ln_b.reshape(1, W), w_s, bias)


def _outproj_body(o_ref, sg_ref, x_ref, wa_ref, wb_ref, g_ref, wr_ref, br_ref,
                  x1_ref, h_ref, ri_ref, rg_ref, cnt_ref, tri_scr, carry_scr):
    tm = x_ref.shape[0]

    @pl.when(pl.program_id(0) == 0)
    def _():
        r = lax.broadcasted_iota(jnp.int32, (tm, tm), 0)
        c = lax.broadcasted_iota(jnp.int32, (tm, tm), 1)
        tri_scr[...] = (c < r).astype(BF16)
        carry_scr[...] = jnp.zeros_like(carry_scr)

    x1 = x_ref[...] + _dot(o_ref[...], wa_ref[...]) + _dot(sg_ref[...], wb_ref[...])
    x1_ref[...] = x1
    h = _rms(x1, g_ref[...], RMS_EPS)
    half = h.shape[1] // 2
    h_ref[...] = _pack_bf16_pair(h[:, :half], h[:, half:])

    logits = jnp.dot(h, wr_ref[...], preferred_element_type=F32,
                     precision=lax.Precision.HIGHEST) + br_ref[...]
    lane = lax.broadcasted_iota(jnp.int32, (tm, LANES), 1)
    neg = jnp.float32(-jnp.inf)
    work = jnp.where(lane < N_EXPERTS, logits, neg)
    vals, idxs = [], []
    for _ in range(TOP_K):
        m = jnp.max(work, axis=-1, keepdims=True)
        i = jnp.min(jnp.where(work == m, lane, LANES), axis=-1, keepdims=True)
        vals.append(m)
        idxs.append(i)
        work = jnp.where(lane == i, neg, work)
    es = [jnp.exp(m - vals[0]) for m in vals]
    denom = es[0] + es[1] + es[2] + es[3]

    onehot = jnp.zeros((tm, LANES), F32)
    for i in idxs:
        onehot = onehot + (lane == i).astype(F32)
    before = _dot(tri_scr[...], onehot.astype(BF16)) + carry_scr[0:1, :]
    carry = carry_scr[0:1, :] + jnp.sum(onehot, axis=0, keepdims=True)
    carry_scr[...] = jnp.broadcast_to(carry, carry_scr.shape)
    cnt_ref[...] = jnp.broadcast_to(carry, cnt_ref.shape).astype(jnp.int32)

    ri = jnp.zeros((tm, LANES), jnp.int32)
    rg = jnp.zeros((tm, LANES), F32)
    for kk in range(TOP_K):
        rank = jnp.sum(jnp.where(lane == idxs[kk], before, 0.0), axis=-1, keepdims=True)
        ri = jnp.where(lane == kk, idxs[kk], ri)
        ri = jnp.where(lane == TOP_K + kk, rank.astype(jnp.int32), ri)
        rg = jnp.where(lane == kk, es[kk] / denom, rg)
    ri_ref[...] = ri
    rg_ref[...] = rg


def _outproj(o, sg, x2d, w_out, g, w_router, b_router, tm=512):
    T, D = x2d.shape
    Wh = o.shape[1]
    wa = w_out[:Wh].astype(BF16)
    wb = w_out[Wh:].astype(BF16)
    wr = jnp.zeros((D, LANES), F32).at[:, :N_EXPERTS].set(w_router)
    br = jnp.zeros((1, LANES), F32).at[0, :N_EXPERTS].set(b_router)
    row = lambda i: (i, 0)
    fixed = lambda i: (0, 0)
    return pl.pallas_call(
        _outproj_body,
        grid=(T // tm,),
        in_specs=[pl.BlockSpec((tm, Wh), row), pl.BlockSpec((tm, sg.shape[1]), row),
                  pl.BlockSpec((tm, D), row),
                  pl.BlockSpec(wa.shape, fixed), pl.BlockSpec(wb.shape, fixed),
                  pl.BlockSpec((1, D), fixed), pl.BlockSpec(wr.shape, fixed),
                  pl.BlockSpec(br.shape, fixed)],
        out_specs=[pl.BlockSpec((tm, D), row), pl.BlockSpec((tm, D // 2), row),
                   pl.BlockSpec((tm, LANES), row), pl.BlockSpec((tm, LANES), row),
                   pl.BlockSpec((8, LANES), fixed)],
        out_shape=[jax.ShapeDtypeStruct((T, D), F32), jax.ShapeDtypeStruct((T, D // 2), jnp.uint32),
                   jax.ShapeDtypeStruct((T, LANES), jnp.int32),
                   jax.ShapeDtypeStruct((T, LANES), F32),
                   jax.ShapeDtypeStruct((8, LANES), jnp.int32)],
        scratch_shapes=[pltpu.VMEM((tm, tm), BF16), pltpu.VMEM((8, LANES), F32)],
        compiler_params=_cparams(1),
        name="outproj_router",
    )(o, sg, x2d, wa, wb, g.reshape(1, D), wr, br)


def _dispatch_body(dest_ref, ends_ref, h_ref, xs_ref, zero_scr, sem, zsem):
    tm = h_ref.shape[0]
    base = pl.program_id(0) * (tm * TOP_K)

    @pl.when(pl.program_id(0) == 0)
    def _():
        zero_scr[...] = jnp.zeros_like(zero_scr)

        def last_block(e, op):
            end = ends_ref[e]
            prev = ends_ref[e - 1] if e else 0

            @pl.when(end > prev)
            def _():
                start = pl.multiple_of(end - EXPERT_BLOCK, EXPERT_BLOCK)
                op(pltpu.make_async_copy(zero_scr, xs_ref.at[pl.ds(start, EXPERT_BLOCK)], zsem))

        for e in range(N_EXPERTS):
            last_block(e, lambda cp: cp.start())
        for e in range(N_EXPERTS):
            last_block(e, lambda cp: cp.wait())

        def tail_copy(b):
            start = pl.multiple_of(b * EXPERT_BLOCK, EXPERT_BLOCK)
            return pltpu.make_async_copy(zero_scr, xs_ref.at[pl.ds(start, EXPERT_BLOCK)], zsem)

        first = ends_ref[N_EXPERTS - 1] // EXPERT_BLOCK
        n_blocks = xs_ref.shape[0] // EXPERT_BLOCK
        lax.fori_loop(first, n_blocks, lambda b, c: (tail_copy(b).start(), c)[1], 0)
        lax.fori_loop(first, n_blocks, lambda b, c: (tail_copy(b).wait(), c)[1], 0)

    def row_copy(r, d):
        return pltpu.make_async_copy(h_ref.at[pl.ds(r, 1)], xs_ref.at[pl.ds(d, 1)], sem)

    def issue(r, carry):
        for kk in range(TOP_K):
            row_copy(r, dest_ref[base + r * TOP_K + kk]).start()
        return carry

    lax.fori_loop(0, tm, issue, 0)

    def drain(r, carry):
        for kk in range(TOP_K):
            row_copy(0, 0).wait()
        return carry

    lax.fori_loop(0, tm, drain, 0)


def _dispatch(dest_flat, pad_ends, h, n_rows, tm=256):
    T, W = h.shape
    return pl.pallas_call(
        _dispatch_body,
        grid_spec=pltpu.PrefetchScalarGridSpec(
            num_scalar_prefetch=2,
            grid=(T // tm,),
            in_specs=[pl.BlockSpec((tm, W), lambda i, dest, ends: (i, 0))],
            out_specs=pl.BlockSpec(memory_space=pl.ANY),
            scratch_shapes=[pltpu.VMEM((EXPERT_BLOCK, W), h.dtype),
                            pltpu.SemaphoreType.DMA(()), pltpu.SemaphoreType.DMA(())]),
        out_shape=jax.ShapeDtypeStruct((n_rows, W), h.dtype),
        compiler_params=_cparams(1),
        name="dispatch",
    )(dest_flat, pad_ends, h)


def _swiglu_deinterleaved(gu, fc):
    even = lax.broadcasted_iota(jnp.int32, (gu.shape[0], LANES), 1) % 2 == 0
    acts = []
    for c in range(fc // LANES):
        a = gu[:, c * LANES:(c + 1) * LANES]
        b = gu[:, fc + c * LANES:fc + (c + 1) * LANES]
        gate = jnp.where(even, a, pltpu.roll(b, 1, axis=1))
        up = jnp.where(even, pltpu.roll(a, LANES - 1, axis=1), b)
        gate = jnp.minimum(gate, SWIGLU_LIMIT)
        up = jnp.clip(up, -SWIGLU_LIMIT, SWIGLU_LIMIT)
        acts.append(((up + 1.0) * gate * jax.nn.sigmoid(SWIGLU_ALPHA * gate)).astype(BF16))
    return jnp.concatenate(acts, axis=1)


def _interleave_rows_bf16(a, b):
    return pltpu.bitcast(_pack_bf16_pair(a, b), BF16)


def _expert_body(ie_ref, ir_ref, in_ref, nu_ref,
                 xs_ref, wgu_ref, wdn_ref, bgu_ref, bdn_ref, ys_ref,
                 xres, act, wgu_bf, wdn_bf, ystage, sem_x, sem_y, *, n_gu, n_dn):
    del ie_ref
    i = pl.program_id(0)
    j = pl.program_id(1)
    nrb = in_ref[i]
    row0 = ir_ref[i]
    fc = act.shape[2]
    nc = ystage.shape[2]
    blk = EXPERT_BLOCK

    def rows(rb, first=0):
        return pl.ds(pl.multiple_of(first + rb * blk, blk), blk)

    @pl.when((i == 0) & (j == 0))
    def _():
        ystage[0] = jnp.zeros(ystage.shape[1:], F32)
        n_blocks = ys_ref.shape[0] // blk

        def tail_copy(b, cc):
            return pltpu.make_async_copy(
                ystage.at[0], ys_ref.at[rows(b), pl.ds(cc * nc, nc)], sem_y.at[0])

        def tail_start(b, c):
            for cc in range(n_dn):
                tail_copy(b, cc).start()
            return c

        def tail_wait(b, c):
            for cc in range(n_dn):
                tail_copy(b, cc).wait()
            return c

        lax.fori_loop(nu_ref[0], n_blocks, tail_start, 0)
        lax.fori_loop(nu_ref[0], n_blocks, tail_wait, 0)

    def x_copy(rb):
        return pltpu.make_async_copy(xs_ref.at[rows(rb, row0)], xres.at[rows(rb)], sem_x)

    def for_blocks(fn):
        lax.fori_loop(0, nrb, lambda rb, c: (fn(rb), c)[1], 0)

    @pl.when((nrb > 0) & (j < n_gu))
    def _():
        @pl.when(j == 0)
        def _():
            for_blocks(lambda rb: x_copy(rb).start())

        wgu_bf[...] = wgu_ref[...].astype(BF16)

        @pl.when(j == 0)
        def _():
            for_blocks(lambda rb: x_copy(rb).wait())

        def block(rb):
            x = jnp.concatenate(_unpack_bf16_pair(xres[rows(rb), :]), axis=1)
            gu = _dot(x, wgu_bf[...]) + bgu_ref[...]
            act[j, rows(rb), :] = _swiglu_deinterleaved(gu, fc)

        for_blocks(block)

    @pl.when((nrb > 0) & (j >= n_gu))
    def _():
        jd = j - n_gu
        n_slots = ystage.shape[0]
        for jf in range(n_gu):
            for c in range(fc // LANES):
                a0 = jf * fc + c * (LANES // 2)
                b0 = a0 + fc // 2
                wdn_bf[jf * fc + c * LANES:jf * fc + (c + 1) * LANES, :] = _interleave_rows_bf16(
                    wdn_ref[a0:a0 + LANES // 2, :], wdn_ref[b0:b0 + LANES // 2, :])

        def y_copy(rb):
            slot = rb % n_slots
            return pltpu.make_async_copy(
                ystage.at[slot], ys_ref.at[rows(rb, row0), pl.ds(pl.multiple_of(jd * nc, nc), nc)],
                sem_y.at[slot])

        def block(rb):
            @pl.when(rb >= n_slots)
            def _():
                y_copy(rb - n_slots).wait()

            a = jnp.concatenate([act[jf, rows(rb), :] for jf in range(n_gu)], axis=1)
            ystage[rb % n_slots] = _dot(a, wdn_bf[...]) + bdn_ref[...]
            y_copy(rb).start()

        for_blocks(block)
        for back in range(1, n_slots + 1):
            @pl.when(nrb >= back)
            def _():
                y_copy(nrb - back).wait()


def _experts(items, xs, w_gate_up, w_down, b_gate_up, b_down, fc, nc):
    item_e, item_row0, item_nrb, n_used = items
    n_rows = xs.shape[0]
    E, D, F2 = w_gate_up.shape
    F = F2 // 2
    n_gu = F // fc
    n_dn = D // nc
    n_items = item_e.shape[0]

    def gu_chunk(i, j, ie, ir, inr, nu):
        return jnp.where(inr[i] > 0, jnp.minimum(j, n_gu - 1), n_gu - 1)

    def dn_chunk(i, j, ie, ir, inr, nu):
        return jnp.where(inr[i] > 0, jnp.maximum(j - n_gu, 0), n_dn - 1)

    return pl.pallas_call(
        functools.partial(_expert_body, n_gu=n_gu, n_dn=n_dn),
        grid_spec=pltpu.PrefetchScalarGridSpec(
            num_scalar_prefetch=4,
            grid=(n_items, n_gu + n_dn),
            in_specs=[
                pl.BlockSpec(memory_space=pl.ANY),
                pl.BlockSpec((None, D, 2 * fc), lambda i, j, ie, *a: (ie[i], 0, gu_chunk(i, j, ie, *a))),
                pl.BlockSpec((None, F, nc), lambda i, j, ie, *a: (ie[i], 0, dn_chunk(i, j, ie, *a))),
                pl.BlockSpec((None, 1, 2 * fc), lambda i, j, ie, *a: (ie[i], 0, gu_chunk(i, j, ie, *a))),
                pl.BlockSpec((None, 1, nc), lambda i, j, ie, *a: (ie[i], 0, dn_chunk(i, j, ie, *a))),
            ],
            out_specs=pl.BlockSpec(memory_space=pl.ANY),
            scratch_shapes=[
                pltpu.VMEM((EXPERT_ROWS, D // 2), jnp.uint32),
                pltpu.VMEM((n_gu, EXPERT_ROWS, fc), BF16),
                pltpu.VMEM((D, 2 * fc), BF16), pltpu.VMEM((F, nc), BF16),
                pltpu.VMEM((EXPERT_YSLOTS, EXPERT_BLOCK, nc), F32),
                pltpu.SemaphoreType.DMA(()), pltpu.SemaphoreType.DMA((EXPERT_YSLOTS,))]),
        out_shape=jax.ShapeDtypeStruct((n_rows, D), F32),
        compiler_params=_cparams(2),
        name="experts",
    )(item_e, item_row0, item_nrb, n_used, xs, w_gate_up, w_down, b_gate_up[:, None, :],
      b_down[:, None, :])


def _combine_body(dest_ref, x1_ref, rg_ref, p_ref, ys_ref, gp_ref, wpg_ref, wpp_ref, gf_ref,
                  out_ref, buf, sem):
    tm = x1_ref.shape[0]
    base = pl.program_id(0) * (tm * TOP_K)

    def row_copy(r, kk, d):
        return pltpu.make_async_copy(ys_ref.at[pl.ds(d, 1)], buf.at[kk, pl.ds(r, 1)], sem)

    def issue(r, carry):
        for kk in range(TOP_K):
            row_copy(r, kk, dest_ref[base + r * TOP_K + kk]).start()
        return carry

    lax.fori_loop(0, tm, issue, 0)

    def drain(r, carry):
        for kk in range(TOP_K):
            row_copy(0, kk, 0).wait()
        return carry

    lax.fori_loop(0, tm, drain, 0)

    gates = rg_ref[...]
    x2 = x1_ref[...]
    for kk in range(TOP_K):
        x2 = x2 + gates[:, kk:kk + 1] * buf[kk]
    hp = _rms(x2, gp_ref[...], RMS_EPS).astype(BF16)
    gate = jax.nn.sigmoid(_dot(hp, wpg_ref[...]))
    proj = _dot(p_ref[...].astype(BF16), wpp_ref[...])
    x3 = x2 + gate * proj
    out_ref[...] = _rms(x3, gf_ref[...], RMS_EPS)


def _combine(dest_flat, x1, rg, p2d, ys, g_ple, w_gate, w_proj, g_final, tm=256):
    T, D = x1.shape
    P = p2d.shape[1]
    row = lambda i, dest: (i, 0)
    fixed = lambda i, dest: (0, 0)
    return pl.pallas_call(
        _combine_body,
        grid_spec=pltpu.PrefetchScalarGridSpec(
            num_scalar_prefetch=1,
            grid=(T // tm,),
            in_specs=[pl.BlockSpec((tm, D), row), pl.BlockSpec((tm, LANES), row),
                      pl.BlockSpec((tm, P), row), pl.BlockSpec(memory_space=pl.ANY),
                      pl.BlockSpec((1, D), fixed), pl.BlockSpec((D, D), fixed),
                      pl.BlockSpec((P, D), fixed), pl.BlockSpec((1, D), fixed)],
            out_specs=pl.BlockSpec((tm, D), row),
            scratch_shapes=[pltpu.VMEM((TOP_K, tm, D), F32), pltpu.SemaphoreType.DMA(())]),
        out_shape=jax.ShapeDtypeStruct((T, D), F32),
        compiler_params=_cparams(1),
        name="combine_ple",
    )(dest_flat, x1, rg, p2d, ys, g_ple.reshape(1, D), w_gate.astype(BF16),
      w_proj.astype(BF16), g_final.reshape(1, D))


def _routing_tables(ri, counts_block, n_items):
    counts = counts_block[0, :N_EXPERTS]
    padded = (counts + EXPERT_BLOCK - 1) // EXPERT_BLOCK * EXPERT_BLOCK
    pad_ends = jnp.cumsum(padded)
    pad_starts = pad_ends - padded
    idx = ri[:, :TOP_K]
    rank = ri[:, TOP_K:2 * TOP_K]
    dest = (pad_starts[idx] + rank).reshape(-1).astype(jnp.int32)

    per_expert = (padded + EXPERT_ROWS - 1) // EXPERT_ROWS
    item_ends = jnp.cumsum(per_expert)
    total = item_ends[-1]
    ids = jnp.arange(n_items, dtype=jnp.int32)
    e_of = jnp.sum(jnp.minimum(ids, total - 1)[:, None] >= item_ends[None, :], axis=1)
    e_of = jnp.clip(e_of, 0, N_EXPERTS - 1).astype(jnp.int32)
    local = ids - (item_ends - per_expert)[e_of]
    active = ids < total
    row0 = jnp.where(active, pad_starts[e_of] + local * EXPERT_ROWS, 0)
    left = jnp.clip(padded[e_of] - local * EXPERT_ROWS, 0, EXPERT_ROWS)
    nrb = jnp.where(active, left // EXPERT_BLOCK, 0)
    n_used = pad_ends[-1:] // EXPERT_BLOCK
    i32 = lambda a: a.astype(jnp.int32)
    return dest, i32(pad_ends), (e_of, i32(row0), i32(nrb), i32(n_used))


def _layer(x2d, p2d, B, S, norm_mix, w_in, lb, hgrn_out_norm, gmlp_ln_g, gmlp_ln_b, w_spatial,
           b_spatial, w_out, norm_ffn, w_router, b_router, w_gate_up, b_gate_up, w_down, b_down,
           norm_ple, w_ple_gate, w_ple_proj, g_out):
    T = x2d.shape[0]
    z = _inproj(x2d, norm_mix, w_in.astype(BF16))
    o = _hgrn(z, lb, hgrn_out_norm, B, S)
    sg = _gmlp(z, gmlp_ln_g, gmlp_ln_b, w_spatial, b_spatial)
    x1, h2, ri, rg, counts = _outproj(o, sg, x2d, w_out, norm_ffn, w_router, b_router)

    n_rows = T * TOP_K + N_EXPERTS * EXPERT_BLOCK
    n_items = N_EXPERTS + T * TOP_K // EXPERT_ROWS
    dest, pad_ends, items = _routing_tables(ri, counts, n_items)
    xs = _dispatch(dest, pad_ends, h2, n_rows)
    ys = _experts(items, xs, w_gate_up, w_down, b_gate_up, b_down, EXPERT_FC, EXPERT_NC)
    return _combine(dest, x1, rg, p2d, ys, norm_ple, w_ple_gate, w_ple_proj, g_out)


def kernel(x, p, norm_mix, w_in, lb_logits, hgrn_out_norm, gmlp_ln_g, gmlp_ln_b, w_spatial, b_spatial, w_out, norm_ffn, w_router, b_router, w_gate_up, b_gate_up, w_down, b_down, norm_ple, w_ple_gate, w_ple_proj, norm_final):
    B, S, D = x.shape
    depth = p.shape[0]
    assert depth == 1, "the final norm is fused into the last layer's combine kernel"
    lower_bounds = jnp.cumsum(jax.nn.softmax(lb_logits.astype(F32), axis=0), axis=0)
    out = _layer(x.reshape(B * S, D), p[0].reshape(B * S, -1), B, S, norm_mix[0], w_in[0],
                 lower_bounds[0], hgrn_out_norm[0], gmlp_ln_g[0], gmlp_ln_b[0], w_spatial[0],
                 b_spatial[0], w_out[0], norm_ffn[0], w_router[0], b_router[0], w_gate_up[0],
                 b_gate_up[0], w_down[0], b_down[0], norm_ple[0], w_ple_gate[0], w_ple_proj[0],
                 norm_final)
    return out.reshape(B, S, D)
```

```python
import functools
import math

import numpy as np
import jax
import jax.numpy as jnp
from jax import lax
from jax.experimental import pallas as pl
from jax.experimental.pallas import tpu as pltpu

F32 = jnp.float32
BF16 = jnp.bfloat16

HG_HEADS = 8
HG_D = 128
GM_GROUPS = 8
GM_CH = 128
GM_CHUNK = 128
N_EXPERTS = 32
TOP_K = 4
SWIGLU_LIMIT = 7.0
SWIGLU_ALPHA = 1.702
RMS_EPS = 1e-6
LN_EPS = 1e-5

LANES = 128
HGRN_TILE = 256
EXPERT_BLOCK = 256
EXPERT_ROWS = 2560
EXPERT_FC = 512
EXPERT_NC = 512
EXPERT_YSLOTS = 2
VMEM_LIMIT = 56 * 1024 * 1024


def _cparams(n_axes):
    return pltpu.CompilerParams(dimension_semantics=("arbitrary",) * n_axes,
                                vmem_limit_bytes=VMEM_LIMIT)


def _rms(x, g, eps):
    return x * lax.rsqrt(jnp.mean(x * x, axis=-1, keepdims=True) + eps) * g


def _dot(a, b):
    return jnp.dot(a, b, preferred_element_type=F32)


def _dot_nt(a, b):
    return lax.dot_general(a, b, (((1,), (1,)), ((), ())), preferred_element_type=F32)


def _bf16_bits(x):
    return pltpu.bitcast(x.astype(BF16).astype(F32), jnp.uint32) & jnp.uint32(0xFFFF0000)


def _pack_bf16_pair(lo, hi):
    return (_bf16_bits(lo) >> 16) | _bf16_bits(hi)


def _unpack_bf16_pair(w):
    lo = pltpu.bitcast(w << 16, F32).astype(BF16)
    hi = pltpu.bitcast(w & jnp.uint32(0xFFFF0000), F32).astype(BF16)
    return lo, hi


def _gelu(x):
    return 0.5 * x * (1.0 + lax.erf(x * (1.0 / math.sqrt(2.0))))


def _inproj_body(x_ref, g_ref, w_ref, z_ref, h_scr):
    @pl.when(pl.program_id(1) == 0)
    def _():
        h_scr[...] = _rms(x_ref[...], g_ref[...], RMS_EPS).astype(BF16)

    z_ref[...] = _dot(h_scr[...], w_ref[...])


def _inproj(x2d, g, w, tm=1024, tn=1024):
    T, D = x2d.shape
    N = w.shape[1]
    return pl.pallas_call(
        _inproj_body,
        grid=(T // tm, N // tn),
        in_specs=[pl.BlockSpec((tm, D), lambda i, j: (i, 0)),
                  pl.BlockSpec((1, D), lambda i, j: (0, 0)),
                  pl.BlockSpec((D, tn), lambda i, j: (0, j))],
        out_specs=pl.BlockSpec((tm, tn), lambda i, j: (i, j)),
        out_shape=jax.ShapeDtypeStruct((T, N), F32),
        scratch_shapes=[pltpu.VMEM((tm, D), BF16)],
        compiler_params=_cparams(2),
        name="inproj",
    )(x2d, g.reshape(1, D), w)


HGRN_MIN_BROADCAST = 8


def _hgrn_constants(C):
    nlev = int(math.log2(C))
    r = np.arange(C)
    mats = [(r[None, :] <= r[:, None])]
    level = np.full((C, C), -1, np.int32)
    level[r, r] = 0
    for l in range(1, nlev + 1):
        L = C >> l
        pos = r % (2 * L)
        mid = (r // (2 * L)) * (2 * L) + L - 1
        second = pos >= L
        if L < HGRN_MIN_BROADCAST:
            mats.append(np.where(second[:, None],
                                 (r[None, :] > mid[:, None]) & (r[None, :] <= r[:, None]),
                                 (r[None, :] > r[:, None]) & (r[None, :] <= mid[:, None])))
        same = (r[:, None] // (2 * L)) == (r[None, :] // (2 * L))
        level[same & second[:, None] & (~second)[None, :]] = l
    h = C // 2
    assert np.array_equal(level[:h, :h], level[h:, h:])
    w = np.concatenate(mats, axis=0).astype(np.float32)
    return jnp.asarray(w, BF16), jnp.asarray(level[:h, :h]), nlev


def _hgrn_body(q_ref, f_ref, v_ref, og_ref, lb_ref, gn_ref, w_ref, lv_ref, o_ref, st_scr, *, nlev):
    C = q_ref.shape[0]
    h = C // 2

    @pl.when(pl.program_id(2) == 0)
    def _():
        st_scr[...] = jnp.zeros_like(st_scr)

    q = q_ref[...]
    q = q * jax.nn.sigmoid(q)
    lb = lb_ref[...]
    f = lb + (1.0 - lb) * jax.nn.sigmoid(f_ref[...])
    g = jnp.log(f)
    k = 1.0 - f
    vb = v_ref[...].astype(BF16)

    g1 = g.astype(BF16)
    r1 = g - g1.astype(F32)
    g2 = r1.astype(BF16)
    g3 = (r1 - g2.astype(F32)).astype(BF16)
    gcat = jnp.concatenate([g1, g2, g3], axis=1)

    def range_sum(i):
        e3 = _dot(w_ref[i * C:(i + 1) * C, :], gcat)
        return e3[:, :HG_D] + e3[:, HG_D:2 * HG_D] + e3[:, 2 * HG_D:]

    b = range_sum(0)
    b_last = b[C - 1:C, :]

    def level_exponent(l):
        L = C >> l
        if L < HGRN_MIN_BROADCAST:
            return range_sum(1 + (l - (nlev + 1 - int(math.log2(HGRN_MIN_BROADCAST)))))
        mids = [jnp.broadcast_to(b[s + L - 1:s + L, :], (2 * L, HG_D)) for s in range(0, C, 2 * L)]
        mid = mids[0] if len(mids) == 1 else jnp.concatenate(mids, axis=0)
        return -jnp.abs(b - mid)

    def scaled(l):
        e = jnp.exp(level_exponent(l))
        return (q * e).astype(BF16), (k * e).astype(BF16)

    ql, kl = scaled(1)
    a_cross = _dot_nt(ql[h:], kl[:h])
    lv = lv_ref[...]
    qb, kb = q.astype(BF16), k.astype(BF16)
    a_top = jnp.where(lv == 0, _dot_nt(qb[:h], kb[:h]), 0.0)
    a_bot = jnp.where(lv == 0, _dot_nt(qb[h:], kb[h:]), 0.0)
    for l in range(2, nlev + 1):
        ql, kl = scaled(l)
        a_top = jnp.where(lv == l, _dot_nt(ql[:h], kl[:h]), a_top)
        a_bot = jnp.where(lv == l, _dot_nt(ql[h:], kl[h:]), a_bot)

    st = st_scr[...]
    o_top = _dot(a_top.astype(BF16), vb[:h])
    o_bot = _dot(a_cross.astype(BF16), vb[:h]) + _dot(a_bot.astype(BF16), vb[h:])
    o = _dot_nt((q * jnp.exp(b)).astype(BF16), st.astype(BF16))
    o = o + jnp.concatenate([o_top, o_bot], axis=0)

    khat = (k * jnp.exp(b_last - b)).astype(BF16)
    st_scr[...] = st * jnp.exp(b_last) + _dot(v_ref[...].T.astype(BF16), khat)

    o = _rms(o, gn_ref[...], RMS_EPS)
    o_ref[...] = (o * jax.nn.sigmoid(og_ref[...])).astype(BF16)


def _hgrn(z, lb, gn, B, S):
    C = HGRN_TILE
    H = HG_HEADS
    n_s = S // C
    w, lv, nlev = _hgrn_constants(C)
    row = lambda b, h, s: b * n_s + s
    col = lambda off: pl.BlockSpec((C, HG_D), lambda b, h, s: (row(b, h, s), off * H + h))
    per_head = pl.BlockSpec((None, 1, HG_D), lambda b, h, s: (h, 0, 0))
    return pl.pallas_call(
        functools.partial(_hgrn_body, nlev=nlev),
        grid=(B, H, n_s),
        in_specs=[col(0), col(1), col(2), col(3), per_head, per_head,
                  pl.BlockSpec(w.shape, lambda b, h, s: (0, 0)),
                  pl.BlockSpec(lv.shape, lambda b, h, s: (0, 0))],
        out_specs=pl.BlockSpec((C, HG_D), lambda b, h, s: (row(b, h, s), h)),
        out_shape=jax.ShapeDtypeStruct((B * S, H * HG_D), BF16),
        scratch_shapes=[pltpu.VMEM((HG_D, HG_D), F32)],
        compiler_params=_cparams(3),
        name="hgrn",
    )(z, z, z, z, lb.reshape(H, 1, HG_D), gn.reshape(H, 1, HG_D), w, lv)


def _gmlp_body(u_ref, v_ref, lng_ref, lnb_ref, ws_ref, bias_ref, o_ref, w_scr):
    @pl.when(pl.program_id(0) == 0)
    def _():
        r = lax.broadcasted_iota(jnp.int32, ws_ref.shape, 1)
        c = lax.broadcasted_iota(jnp.int32, ws_ref.shape, 2)
        w_scr[...] = jnp.where(r >= c, ws_ref[...], 0.0).astype(BF16)

    v = _gelu(v_ref[...])
    mu = jnp.mean(v, axis=-1, keepdims=True)
    vc = v - mu
    vn = vc * lax.rsqrt(jnp.mean(vc * vc, axis=-1, keepdims=True) + LN_EPS)
    vn = (vn * lng_ref[...] + lnb_ref[...]).astype(BF16)
    n_chunks = u_ref.shape[0] // GM_CHUNK
    for c in range(n_chunks):
        rows = slice(c * GM_CHUNK, (c + 1) * GM_CHUNK)
        for g in range(GM_GROUPS):
            cols = slice(g * GM_CH, (g + 1) * GM_CH)
            mixed = _dot(w_scr[g], vn[rows, cols]) + bias_ref[:, cols]
            o_ref[rows, cols] = (_gelu(u_ref[rows, cols]) * mixed).astype(BF16)


def _gmlp(z, ln_g, ln_b, w_s, b_s, tg=512):
    T = z.shape[0]
    W = GM_GROUPS * GM_CH
    u_col = 4 * HG_HEADS * HG_D // W
    bias = jnp.repeat(b_s.T, GM_CH, axis=1)
    return pl.pallas_call(
        _gmlp_body,
        grid=(T // tg,),
        in_specs=[pl.BlockSpec((tg, W), lambda i: (i, u_col)),
                  pl.BlockSpec((tg, W), lambda i: (i, u_col + 1)),
                  pl.BlockSpec((1, W), lambda i: (0, 0)),
                  pl.BlockSpec((1, W), lambda i: (0, 0)),
                  pl.BlockSpec(w_s.shape, lambda i: (0, 0, 0)),
                  pl.BlockSpec(bias.shape, lambda i: (0, 0))],
        out_specs=pl.BlockSpec((tg, W), lambda i: (i, 0)),
        out_shape=jax.ShapeDtypeStruct((T, W), BF16),
        scratch_shapes=[pltpu.VMEM(w_s.shape, BF16)],
        compiler_params=_cparams(1),
        name="gmlp",
    )(z, z, ln_g.reshape(1, W), ln_b.reshape(1, W), w_s, bias)


def _outproj_body(o_ref, sg_ref, x_ref, wa_ref, wb_ref, g_ref, wr_ref, br_ref,
                  x1_ref, h_ref, ri_ref, rg_ref, cnt_ref, tri_scr, carry_scr):
    tm = x_ref.shape[0]

    @pl.when(pl.program_id(0) == 0)
    def _():
        r = lax.broadcasted_iota(jnp.int32, (tm, tm), 0)
        c = lax.broadcasted_iota(jnp.int32, (tm, tm), 1)
        tri_scr[...] = (c < r).astype(BF16)
        carry_scr[...] = jnp.zeros_like(carry_scr)

    x1 = x_ref[...] + _dot(o_ref[...], wa_ref[...]) + _dot(sg_ref[...], wb_ref[...])
    x1_ref[...] = x1
    h = _rms(x1, g_ref[...], RMS_EPS)
    half = h.shape[1] // 2
    h_ref[...] = _pack_bf16_pair(h[:, :half], h[:, half:])

    logits = jnp.dot(h, wr_ref[...], preferred_element_type=F32,
                     precision=lax.Precision.HIGHEST) + br_ref[...]
    lane = lax.broadcasted_iota(jnp.int32, (tm, LANES), 1)
    neg = jnp.float32(-jnp.inf)
    work = jnp.where(lane < N_EXPERTS, logits, neg)
    vals, idxs = [], []
    for _ in range(TOP_K):
        m = jnp.max(work, axis=-1, keepdims=True)
        i = jnp.min(jnp.where(work == m, lane, LANES), axis=-1, keepdims=True)
        vals.append(m)
        idxs.append(i)
        work = jnp.where(lane == i, neg, work)
    es = [jnp.exp(m - vals[0]) for m in vals]
    denom = es[0] + es[1] + es[2] + es[3]

    onehot = jnp.zeros((tm, LANES), F32)
    for i in idxs:
        onehot = onehot + (lane == i).astype(F32)
    before = _dot(tri_scr[...], onehot.astype(BF16)) + carry_scr[0:1, :]
    carry = carry_scr[0:1, :] + jnp.sum(onehot, axis=0, keepdims=True)
    carry_scr[...] = jnp.broadcast_to(carry, carry_scr.shape)
    cnt_ref[...] = jnp.broadcast_to(carry, cnt_ref.shape).astype(jnp.int32)

    ri = jnp.zeros((tm, LANES), jnp.int32)
    rg = jnp.zeros((tm, LANES), F32)
    for kk in range(TOP_K):
        rank = jnp.sum(jnp.where(lane == idxs[kk], before, 0.0), axis=-1, keepdims=True)
        ri = jnp.where(lane == kk, idxs[kk], ri)
        ri = jnp.where(lane == TOP_K + kk, rank.astype(jnp.int32), ri)
        rg = jnp.where(lane == kk, es[kk] / denom, rg)
    ri_ref[...] = ri
    rg_ref[...] = rg


def _outproj(o, sg, x2d, w_out, g, w_router, b_router, tm=512):
    T, D = x2d.shape
    Wh = o.shape[1]
    wa = w_out[:Wh].astype(BF16)
    wb = w_out[Wh:].astype(BF16)
    wr = jnp.zeros((D, LANES), F32).at[:, :N_EXPERTS].set(w_router)
    br = jnp.zeros((1, LANES), F32).at[0, :N_EXPERTS].set(b_router)
    row = lambda i: (i, 0)
    fixed = lambda i: (0, 0)
    return pl.pallas_call(
        _outproj_body,
        grid=(T // tm,),
        in_specs=[pl.BlockSpec((tm, Wh), row), pl.BlockSpec((tm, sg.shape[1]), row),
                  pl.BlockSpec((tm, D), row),
                  pl.BlockSpec(wa.shape, fixed), pl.BlockSpec(wb.shape, fixed),
                  pl.BlockSpec((1, D), fixed), pl.BlockSpec(wr.shape, fixed),
                  pl.BlockSpec(br.shape, fixed)],
        out_specs=[pl.BlockSpec((tm, D), row), pl.BlockSpec((tm, D // 2), row),
                   pl.BlockSpec((tm, LANES), row), pl.BlockSpec((tm, LANES), row),
                   pl.BlockSpec((8, LANES), fixed)],
        out_shape=[jax.ShapeDtypeStruct((T, D), F32), jax.ShapeDtypeStruct((T, D // 2), jnp.uint32),
                   jax.ShapeDtypeStruct((T, LANES), jnp.int32),
                   jax.ShapeDtypeStruct((T, LANES), F32),
                   jax.ShapeDtypeStruct((8, LANES), jnp.int32)],
        scratch_shapes=[pltpu.VMEM((tm, tm), BF16), pltpu.VMEM((8, LANES), F32)],
        compiler_params=_cparams(1),
        name="outproj_router",
    )(o, sg, x2d, wa, wb, g.reshape(1, D), wr, br)


def _dispatch_body(dest_ref, ends_ref, h_ref, xs_ref, zero_scr, sem, zsem):
    tm = h_ref.shape[0]
    base = pl.program_id(0) * (tm * TOP_K)

    @pl.when(pl.program_id(0) == 0)
    def _():
        zero_scr[...] = jnp.zeros_like(zero_scr)

        def last_block(e, op):
            end = ends_ref[e]
            prev = ends_ref[e - 1] if e else 0

            @pl.when(end > prev)
            def _():
                start = pl.multiple_of(end - EXPERT_BLOCK, EXPERT_BLOCK)
                op(pltpu.make_async_copy(zero_scr, xs_ref.at[pl.ds(start, EXPERT_BLOCK)], zsem))

        for e in range(N_EXPERTS):
            last_block(e, lambda cp: cp.start())
        for e in range(N_EXPERTS):
            last_block(e, lambda cp: cp.wait())

        def tail_copy(b):
            start = pl.multiple_of(b * EXPERT_BLOCK, EXPERT_BLOCK)
            return pltpu.make_async_copy(zero_scr, xs_ref.at[pl.ds(start, EXPERT_BLOCK)], zsem)

        first = ends_ref[N_EXPERTS - 1] // EXPERT_BLOCK
        n_blocks = xs_ref.shape[0] // EXPERT_BLOCK
        lax.fori_loop(first, n_blocks, lambda b, c: (tail_copy(b).start(), c)[1], 0)
        lax.fori_loop(first, n_blocks, lambda b, c: (tail_copy(b).wait(), c)[1], 0)

    def row_copy(r, d):
        return pltpu.make_async_copy(h_ref.at[pl.ds(r, 1)], xs_ref.at[pl.ds(d, 1)], sem)

    def issue(r, carry):
        for kk in range(TOP_K):
            row_copy(r, dest_ref[base + r * TOP_K + kk]).start()
        return carry

    lax.fori_loop(0, tm, issue, 0)

    def drain(r, carry):
        for kk in range(TOP_K):
            row_copy(0, 0).wait()
        return carry

    lax.fori_loop(0, tm, drain, 0)


def _dispatch(dest_flat, pad_ends, h, n_rows, tm=256):
    T, W = h.shape
    return pl.pallas_call(
        _dispatch_body,
        grid_spec=pltpu.PrefetchScalarGridSpec(
            num_scalar_prefetch=2,
            grid=(T // tm,),
            in_specs=[pl.BlockSpec((tm, W), lambda i, dest, ends: (i, 0))],
            out_specs=pl.BlockSpec(memory_space=pl.ANY),
            scratch_shapes=[pltpu.VMEM((EXPERT_BLOCK, W), h.dtype),
                            pltpu.SemaphoreType.DMA(()), pltpu.SemaphoreType.DMA(())]),
        out_shape=jax.ShapeDtypeStruct((n_rows, W), h.dtype),
        compiler_params=_cparams(1),
        name="dispatch",
    )(dest_flat, pad_ends, h)


def _swiglu_deinterleaved(gu, fc):
    even = lax.broadcasted_iota(jnp.int32, (gu.shape[0], LANES), 1) % 2 == 0
    acts = []
    for c in range(fc // LANES):
        a = gu[:, c * LANES:(c + 1) * LANES]
        b = gu[:, fc + c * LANES:fc + (c + 1) * LANES]
        gate = jnp.where(even, a, pltpu.roll(b, 1, axis=1))
        up = jnp.where(even, pltpu.roll(a, LANES - 1, axis=1), b)
        gate = jnp.minimum(gate, SWIGLU_LIMIT)
        up = jnp.clip(up, -SWIGLU_LIMIT, SWIGLU_LIMIT)
        acts.append(((up + 1.0) * gate * jax.nn.sigmoid(SWIGLU_ALPHA * gate)).astype(BF16))
    return jnp.concatenate(acts, axis=1)


def _interleave_rows_bf16(a, b):
    return pltpu.bitcast(_pack_bf16_pair(a, b), BF16)


def _expert_body(ie_ref, ir_ref, in_ref, nu_ref,
                 xs_ref, wgu_ref, wdn_ref, bgu_ref, bdn_ref, ys_ref,
                 xres, act, wgu_bf, wdn_bf, ystage, sem_x, sem_y, *, n_gu, n_dn):
    del ie_ref
    i = pl.program_id(0)
    j = pl.program_id(1)
    nrb = in_ref[i]
    row0 = ir_ref[i]
    fc = act.shape[2]
    nc = ystage.shape[2]
    blk = EXPERT_BLOCK

    def rows(rb, first=0):
        return pl.ds(pl.multiple_of(first + rb * blk, blk), blk)

    @pl.when((i == 0) & (j == 0))
    def _():
        ystage[0] = jnp.zeros(ystage.shape[1:], F32)
        n_blocks = ys_ref.shape[0] // blk

        def tail_copy(b, cc):
            return pltpu.make_async_copy(
                ystage.at[0, pl.ds(0, blk)], ys_ref.at[rows(b), pl.ds(cc * nc, nc)], sem_y.at[0])

        def tail_start(b, c):
            for cc in range(n_dn):
                tail_copy(b, cc).start()
            return c

        def tail_wait(b, c):
            for cc in range(n_dn):
                tail_copy(b, cc).wait()
            return c

        lax.fori_loop(nu_ref[0], n_blocks, tail_start, 0)
        lax.fori_loop(nu_ref[0], n_blocks, tail_wait, 0)

    def x_copy(rb):
        return pltpu.make_async_copy(xs_ref.at[rows(rb, row0)], xres.at[rows(rb)], sem_x)

    def for_blocks(fn):
        lax.fori_loop(0, nrb, lambda rb, c: (fn(rb), c)[1], 0)

    n_pairs = nrb // 2
    odd = nrb % 2

    def unit_rows(u, size, first=0):
        return pl.ds(pl.multiple_of(first + u * (2 * blk), blk), size)

    def for_units(fn):
        lax.fori_loop(0, n_pairs, lambda u, c: (fn(u, 2 * blk), c)[1], 0)

        @pl.when(odd == 1)
        def _():
            fn(n_pairs, blk)

    @pl.when((nrb > 0) & (j < n_gu))
    def _():
        @pl.when(j == 0)
        def _():
            for_blocks(lambda rb: x_copy(rb).start())

        wgu_bf[...] = wgu_ref[...].astype(BF16)

        @pl.when(j == 0)
        def _():
            for_blocks(lambda rb: x_copy(rb).wait())

        def unit(u, size):
            x = jnp.concatenate(_unpack_bf16_pair(xres[unit_rows(u, size), :]), axis=1)
            gu = _dot(x, wgu_bf[...]) + bgu_ref[...]
            act[j, unit_rows(u, size), :] = _swiglu_deinterleaved(gu, fc)

        for_units(unit)

    @pl.when((nrb > 0) & (j >= n_gu))
    def _():
        jd = j - n_gu
        n_slots = ystage.shape[0]
        for jf in range(n_gu):
            for c in range(fc // LANES):
                a0 = jf * fc + c * (LANES // 2)
                b0 = a0 + fc // 2
                wdn_bf[jf * fc + c * LANES:jf * fc + (c + 1) * LANES, :] = _interleave_rows_bf16(
                    wdn_ref[a0:a0 + LANES // 2, :], wdn_ref[b0:b0 + LANES // 2, :])

        def y_copy(u, size):
            slot = u % n_slots
            return pltpu.make_async_copy(
                ystage.at[slot, pl.ds(0, size)],
                ys_ref.at[unit_rows(u, size, row0), pl.ds(pl.multiple_of(jd * nc, nc), nc)],
                sem_y.at[slot])

        def unit(u, size):
            @pl.when(u >= n_slots)
            def _():
                y_copy(u - n_slots, 2 * blk).wait()

            a = jnp.concatenate([act[jf, unit_rows(u, size), :] for jf in range(n_gu)], axis=1)
            ystage[u % n_slots, 0:size, :] = _dot(a, wdn_bf[...]) + bdn_ref[...]
            y_copy(u, size).start()

        for_units(unit)
        n_units = n_pairs + odd
        for back in range(1, n_slots + 1):
            @pl.when((n_units >= back) & ((odd == 0) | (back > 1)))
            def _():
                y_copy(n_units - back, 2 * blk).wait()

        @pl.when(odd == 1)
        def _():
            y_copy(n_units - 1, blk).wait()


def _experts(items, xs, w_gate_up, w_down, b_gate_up, b_down, fc, nc):
    item_e, item_row0, item_nrb, n_used = items
    n_rows = xs.shape[0]
    E, D, F2 = w_gate_up.shape
    F = F2 // 2
    n_gu = F // fc
    n_dn = D // nc
    n_items = item_e.shape[0]

    def gu_chunk(i, j, ie, ir, inr, nu):
        return jnp.where(inr[i] > 0, jnp.minimum(j, n_gu - 1), n_gu - 1)

    def dn_chunk(i, j, ie, ir, inr, nu):
        return jnp.where(inr[i] > 0, jnp.maximum(j - n_gu, 0), n_dn - 1)

    return pl.pallas_call(
        functools.partial(_expert_body, n_gu=n_gu, n_dn=n_dn),
        grid_spec=pltpu.PrefetchScalarGridSpec(
            num_scalar_prefetch=4,
            grid=(n_items, n_gu + n_dn),
            in_specs=[
                pl.BlockSpec(memory_space=pl.ANY),
                pl.BlockSpec((None, D, 2 * fc), lambda i, j, ie, *a: (ie[i], 0, gu_chunk(i, j, ie, *a))),
                pl.BlockSpec((None, F, nc), lambda i, j, ie, *a: (ie[i], 0, dn_chunk(i, j, ie, *a))),
                pl.BlockSpec((None, 1, 2 * fc), lambda i, j, ie, *a: (ie[i], 0, gu_chunk(i, j, ie, *a))),
                pl.BlockSpec((None, 1, nc), lambda i, j, ie, *a: (ie[i], 0, dn_chunk(i, j, ie, *a))),
            ],
            out_specs=pl.BlockSpec(memory_space=pl.ANY),
            scratch_shapes=[
                pltpu.VMEM((EXPERT_ROWS, D // 2), jnp.uint32),
                pltpu.VMEM((n_gu, EXPERT_ROWS, fc), BF16),
                pltpu.VMEM((D, 2 * fc), BF16), pltpu.VMEM((F, nc), BF16),
                pltpu.VMEM((EXPERT_YSLOTS, 2 * EXPERT_BLOCK, nc), F32),
                pltpu.SemaphoreType.DMA(()), pltpu.SemaphoreType.DMA((EXPERT_YSLOTS,))]),
        out_shape=jax.ShapeDtypeStruct((n_rows, D), F32),
        compiler_params=_cparams(2),
        name="experts",
    )(item_e, item_row0, item_nrb, n_used, xs, w_gate_up, w_down, b_gate_up[:, None, :],
      b_down[:, None, :])


def _combine_body(dest_ref, x1_ref, rg_ref, p_ref, ys_ref, gp_ref, wpg_ref, wpp_ref, gf_ref,
                  out_ref, buf, sem):
    tm = x1_ref.shape[0]
    base = pl.program_id(0) * (tm * TOP_K)

    def row_copy(r, kk, d):
        return pltpu.make_async_copy(ys_ref.at[pl.ds(d, 1)], buf.at[kk, pl.ds(r, 1)], sem)

    def issue(r, carry):
        for kk in range(TOP_K):
            row_copy(r, kk, dest_ref[base + r * TOP_K + kk]).start()
        return carry

    lax.fori_loop(0, tm, issue, 0)

    def drain(r, carry):
        for kk in range(TOP_K):
            row_copy(0, kk, 0).wait()
        return carry

    lax.fori_loop(0, tm, drain, 0)

    gates = rg_ref[...]
    x2 = x1_ref[...]
    for kk in range(TOP_K):
        x2 = x2 + gates[:, kk:kk + 1] * buf[kk]
    hp = _rms(x2, gp_ref[...], RMS_EPS).astype(BF16)
    gate = jax.nn.sigmoid(_dot(hp, wpg_ref[...]))
    proj = _dot(p_ref[...].astype(BF16), wpp_ref[...])
    x3 = x2 + gate * proj
    out_ref[...] = _rms(x3, gf_ref[...], RMS_EPS)


def _combine(dest_flat, x1, rg, p2d, ys, g_ple, w_gate, w_proj, g_final, tm=256):
    T, D = x1.shape
    P = p2d.shape[1]
    row = lambda i, dest: (i, 0)
    fixed = lambda i, dest: (0, 0)
    return pl.pallas_call(
        _combine_body,
        grid_spec=pltpu.PrefetchScalarGridSpec(
            num_scalar_prefetch=1,
            grid=(T // tm,),
            in_specs=[pl.BlockSpec((tm, D), row), pl.BlockSpec((tm, LANES), row),
                      pl.BlockSpec((tm, P), row), pl.BlockSpec(memory_space=pl.ANY),
                      pl.BlockSpec((1, D), fixed), pl.BlockSpec((D, D), fixed),
                      pl.BlockSpec((P, D), fixed), pl.BlockSpec((1, D), fixed)],
            out_specs=pl.BlockSpec((tm, D), row),
            scratch_shapes=[pltpu.VMEM((TOP_K, tm, D), F32), pltpu.SemaphoreType.DMA(())]),
        out_shape=jax.ShapeDtypeStruct((T, D), F32),
        compiler_params=_cparams(1),
        name="combine_ple",
    )(dest_flat, x1, rg, p2d, ys, g_ple.reshape(1, D), w_gate.astype(BF16),
      w_proj.astype(BF16), g_final.reshape(1, D))


def _routing_tables(ri, counts_block, n_items):
    counts = counts_block[0, :N_EXPERTS]
    padded = (counts + EXPERT_BLOCK - 1) // EXPERT_BLOCK * EXPERT_BLOCK
    pad_ends = jnp.cumsum(padded)
    pad_starts = pad_ends - padded
    idx = ri[:, :TOP_K]
    rank = ri[:, TOP_K:2 * TOP_K]
    dest = (pad_starts[idx] + rank).reshape(-1).astype(jnp.int32)

    per_expert = (padded + EXPERT_ROWS - 1) // EXPERT_ROWS
    item_ends = jnp.cumsum(per_expert)
    total = item_ends[-1]
    ids = jnp.arange(n_items, dtype=jnp.int32)
    e_of = jnp.sum(jnp.minimum(ids, total - 1)[:, None] >= item_ends[None, :], axis=1)
    e_of = jnp.clip(e_of, 0, N_EXPERTS - 1).astype(jnp.int32)
    local = ids - (item_ends - per_expert)[e_of]
    active = ids < total
    row0 = jnp.where(active, pad_starts[e_of] + local * EXPERT_ROWS, 0)
    left = jnp.clip(padded[e_of] - local * EXPERT_ROWS, 0, EXPERT_ROWS)
    nrb = jnp.where(active, left // EXPERT_BLOCK, 0)
    n_used = pad_ends[-1:] // EXPERT_BLOCK
    i32 = lambda a: a.astype(jnp.int32)
    return dest, i32(pad_ends), (e_of, i32(row0), i32(nrb), i32(n_used))


def _layer(x2d, p2d, B, S, norm_mix, w_in, lb, hgrn_out_norm, gmlp_ln_g, gmlp_ln_b, w_spatial,
           b_spatial, w_out, norm_ffn, w_router, b_router, w_gate_up, b_gate_up, w_down, b_down,
           norm_ple, w_ple_gate, w_ple_proj, g_out):
    T = x2d.shape[0]
    z = _inproj(x2d, norm_mix, w_in.astype(BF16))
    o = _hgrn(z, lb, hgrn_out_norm, B, S)
    sg = _gmlp(z, gmlp_ln_g, gmlp_ln_b, w_spatial, b_spatial)
    x1, h2, ri, rg, counts = _outproj(o, sg, x2d, w_out, norm_ffn, w_router, b_router)

    n_rows = T * TOP_K + N_EXPERTS * EXPERT_BLOCK
    n_items = N_EXPERTS + T * TOP_K // EXPERT_ROWS
    dest, pad_ends, items = _routing_tables(ri, counts, n_items)
    xs = _dispatch(dest, pad_ends, h2, n_rows)
    ys = _experts(items, xs, w_gate_up, w_down, b_gate_up, b_down, EXPERT_FC, EXPERT_NC)
    return _combine(dest, x1, rg, p2d, ys, norm_ple, w_ple_gate, w_ple_proj, g_out)


def kernel(x, p, norm_mix, w_in, lb_logits, hgrn_out_norm, gmlp_ln_g, gmlp_ln_b, w_spatial, b_spatial, w_out, norm_ffn, w_router, b_router, w_gate_up, b_gate_up, w_down, b_down, norm_ple, w_ple_gate, w_ple_proj, norm_final):
    B, S, D = x.shape
    depth = p.shape[0]
    assert depth == 1, "the final norm is fused into the last layer's combine kernel"
    lower_bounds = jnp.cumsum(jax.nn.softmax(lb_logits.astype(F32), axis=0), axis=0)
    out = _layer(x.reshape(B * S, D), p[0].reshape(B * S, -1), B, S, norm_mix[0], w_in[0],
                 lower_bounds[0], hgrn_out_norm[0], gmlp_ln_g[0], gmlp_ln_b[0], w_spatial[0],
                 b_spatial[0], w_out[0], norm_ffn[0], w_router[0], b_router[0], w_gate_up[0],
                 b_gate_up[0], w_down[0], b_down[0], norm_ple[0], w_ple_gate[0], w_ple_proj[0],
                 norm_final)
    return out.reshape(B, S, D)
```

```python
import functools
import math

import numpy as np
import jax
import jax.numpy as jnp
from jax import lax
from jax.experimental import pallas as pl
from jax.experimental.pallas import tpu as pltpu

F32 = jnp.float32
BF16 = jnp.bfloat16

HG_HEADS = 8
HG_D = 128
GM_GROUPS = 8
GM_CH = 128
GM_CHUNK = 128
N_EXPERTS = 32
TOP_K = 4
SWIGLU_LIMIT = 7.0
SWIGLU_ALPHA = 1.702
RMS_EPS = 1e-6
LN_EPS = 1e-5

LANES = 128
HGRN_TILE = 256
HGRN_HEADS_PER_STEP = 2
EXPERT_BLOCK = 256
EXPERT_ROWS = 2560
EXPERT_FC = 512
EXPERT_NC = 512
EXPERT_YSLOTS = 2
VMEM_LIMIT = 56 * 1024 * 1024


def _cparams(n_axes):
    return pltpu.CompilerParams(dimension_semantics=("arbitrary",) * n_axes,
                                vmem_limit_bytes=VMEM_LIMIT)


def _rms(x, g, eps):
    return x * lax.rsqrt(jnp.mean(x * x, axis=-1, keepdims=True) + eps) * g


def _dot(a, b):
    return jnp.dot(a, b, preferred_element_type=F32)


def _dot_nt(a, b):
    return lax.dot_general(a, b, (((1,), (1,)), ((), ())), preferred_element_type=F32)


def _bf16_bits(x):
    return pltpu.bitcast(x.astype(BF16).astype(F32), jnp.uint32) & jnp.uint32(0xFFFF0000)


def _pack_bf16_pair(lo, hi):
    return (_bf16_bits(lo) >> 16) | _bf16_bits(hi)


def _unpack_bf16_pair(w):
    lo = pltpu.bitcast(w << 16, F32).astype(BF16)
    hi = pltpu.bitcast(w & jnp.uint32(0xFFFF0000), F32).astype(BF16)
    return lo, hi


def _gelu(x):
    return 0.5 * x * (1.0 + lax.erf(x * (1.0 / math.sqrt(2.0))))


def _inproj_body(x_ref, g_ref, w_ref, z_ref, h_scr):
    @pl.when(pl.program_id(1) == 0)
    def _():
        h_scr[...] = _rms(x_ref[...], g_ref[...], RMS_EPS).astype(BF16)

    z_ref[...] = _dot(h_scr[...], w_ref[...])


def _inproj(x2d, g, w, tm=1024, tn=1024):
    T, D = x2d.shape
    N = w.shape[1]
    return pl.pallas_call(
        _inproj_body,
        grid=(T // tm, N // tn),
        in_specs=[pl.BlockSpec((tm, D), lambda i, j: (i, 0)),
                  pl.BlockSpec((1, D), lambda i, j: (0, 0)),
                  pl.BlockSpec((D, tn), lambda i, j: (0, j))],
        out_specs=pl.BlockSpec((tm, tn), lambda i, j: (i, j)),
        out_shape=jax.ShapeDtypeStruct((T, N), F32),
        scratch_shapes=[pltpu.VMEM((tm, D), BF16)],
        compiler_params=_cparams(2),
        name="inproj",
    )(x2d, g.reshape(1, D), w)


HGRN_MIN_BROADCAST = 8


def _hgrn_constants(C):
    nlev = int(math.log2(C))
    r = np.arange(C)
    mats = [(r[None, :] <= r[:, None])]
    level = np.full((C, C), -1, np.int32)
    level[r, r] = 0
    for l in range(1, nlev + 1):
        L = C >> l
        pos = r % (2 * L)
        mid = (r // (2 * L)) * (2 * L) + L - 1
        second = pos >= L
        if L < HGRN_MIN_BROADCAST:
            mats.append(np.where(second[:, None],
                                 (r[None, :] > mid[:, None]) & (r[None, :] <= r[:, None]),
                                 (r[None, :] > r[:, None]) & (r[None, :] <= mid[:, None])))
        same = (r[:, None] // (2 * L)) == (r[None, :] // (2 * L))
        level[same & second[:, None] & (~second)[None, :]] = l
    h = C // 2
    assert np.array_equal(level[:h, :h], level[h:, h:])
    w = np.concatenate(mats, axis=0).astype(np.float32)
    return jnp.asarray(w, BF16), jnp.asarray(level[:h, :h]), nlev


def _hgrn_head(q, f_logit, v, out_gate, lb, gn, st, w_ref, lv, nlev):
    C = q.shape[0]
    h = C // 2
    q = q * jax.nn.sigmoid(q)
    f = lb + (1.0 - lb) * jax.nn.sigmoid(f_logit)
    g = jnp.log(f)
    k = 1.0 - f
    vb = v.astype(BF16)

    g1 = g.astype(BF16)
    r1 = g - g1.astype(F32)
    g2 = r1.astype(BF16)
    g3 = (r1 - g2.astype(F32)).astype(BF16)
    gcat = jnp.concatenate([g1, g2, g3], axis=1)

    def range_sum(i):
        e3 = _dot(w_ref[i * C:(i + 1) * C, :], gcat)
        return e3[:, :HG_D] + e3[:, HG_D:2 * HG_D] + e3[:, 2 * HG_D:]

    b = range_sum(0)
    b_last = b[C - 1:C, :]

    def level_exponent(l):
        L = C >> l
        if L < HGRN_MIN_BROADCAST:
            return range_sum(1 + (l - (nlev + 1 - int(math.log2(HGRN_MIN_BROADCAST)))))
        mids = [jnp.broadcast_to(b[s + L - 1:s + L, :], (2 * L, HG_D)) for s in range(0, C, 2 * L)]
        mid = mids[0] if len(mids) == 1 else jnp.concatenate(mids, axis=0)
        return -jnp.abs(b - mid)

    def scaled(l):
        e = jnp.exp(level_exponent(l))
        return (q * e).astype(BF16), (k * e).astype(BF16)

    ql, kl = scaled(1)
    a_cross = _dot_nt(ql[h:], kl[:h])
    qb, kb = q.astype(BF16), k.astype(BF16)
    a_top = jnp.where(lv == 0, _dot_nt(qb[:h], kb[:h]), 0.0)
    a_bot = jnp.where(lv == 0, _dot_nt(qb[h:], kb[h:]), 0.0)
    for l in range(2, nlev + 1):
        ql, kl = scaled(l)
        a_top = jnp.where(lv == l, _dot_nt(ql[:h], kl[:h]), a_top)
        a_bot = jnp.where(lv == l, _dot_nt(ql[h:], kl[h:]), a_bot)

    o_top = _dot(a_top.astype(BF16), vb[:h])
    o_bot = _dot(a_cross.astype(BF16), vb[:h]) + _dot(a_bot.astype(BF16), vb[h:])
    o = _dot_nt((q * jnp.exp(b)).astype(BF16), st.astype(BF16))
    o = o + jnp.concatenate([o_top, o_bot], axis=0)

    khat = (k * jnp.exp(b_last - b)).astype(BF16)
    st_new = st * jnp.exp(b_last) + _dot(v.T.astype(BF16), khat)
    o = _rms(o, gn, RMS_EPS) * jax.nn.sigmoid(out_gate)
    return o.astype(BF16), st_new


def _hgrn_body(q_ref, f_ref, v_ref, og_ref, lb_ref, gn_ref, w_ref, lv_ref, o_ref, st_scr, *, nlev):
    @pl.when(pl.program_id(2) == 0)
    def _():
        st_scr[...] = jnp.zeros_like(st_scr)

    lv = lv_ref[...]
    for hh in range(st_scr.shape[0]):
        lanes = slice(hh * HG_D, (hh + 1) * HG_D)
        o, st_new = _hgrn_head(q_ref[:, lanes], f_ref[:, lanes], v_ref[:, lanes], og_ref[:, lanes],
                               lb_ref[hh], gn_ref[hh], st_scr[hh], w_ref, lv, nlev)
        o_ref[:, lanes] = o
        st_scr[hh] = st_new


def _hgrn(z, lb, gn, B, S):
    C = HGRN_TILE
    H = HG_HEADS
    n_s = S // C
    w, lv, nlev = _hgrn_constants(C)
    hp = HGRN_HEADS_PER_STEP
    row = lambda b, h, s: b * n_s + s
    col = lambda off: pl.BlockSpec((C, hp * HG_D),
                                   lambda b, h, s: (row(b, h, s), off * (H // hp) + h))
    per_head = pl.BlockSpec((hp, 1, HG_D), lambda b, h, s: (h, 0, 0))
    return pl.pallas_call(
        functools.partial(_hgrn_body, nlev=nlev),
        grid=(B, H // hp, n_s),
        in_specs=[col(0), col(1), col(2), col(3), per_head, per_head,
                  pl.BlockSpec(w.shape, lambda b, h, s: (0, 0)),
                  pl.BlockSpec(lv.shape, lambda b, h, s: (0, 0))],
        out_specs=pl.BlockSpec((C, hp * HG_D), lambda b, h, s: (row(b, h, s), h)),
        out_shape=jax.ShapeDtypeStruct((B * S, H * HG_D), BF16),
        scratch_shapes=[pltpu.VMEM((hp, HG_D, HG_D), F32)],
        compiler_params=_cparams(3),
        name="hgrn",
    )(z, z, z, z, lb.reshape(H, 1, HG_D), gn.reshape(H, 1, HG_D), w, lv)


def _gmlp_body(u_ref, v_ref, lng_ref, lnb_ref, ws_ref, bias_ref, o_ref, w_scr):
    @pl.when(pl.program_id(0) == 0)
    def _():
        r = lax.broadcasted_iota(jnp.int32, ws_ref.shape, 1)
        c = lax.broadcasted_iota(jnp.int32, ws_ref.shape, 2)
        w_scr[...] = jnp.where(r >= c, ws_ref[...], 0.0).astype(BF16)

    v = _gelu(v_ref[...])
    mu = jnp.mean(v, axis=-1, keepdims=True)
    vc = v - mu
    vn = vc * lax.rsqrt(jnp.mean(vc * vc, axis=-1, keepdims=True) + LN_EPS)
    vn = (vn * lng_ref[...] + lnb_ref[...]).astype(BF16)
    n_chunks = u_ref.shape[0] // GM_CHUNK
    for c in range(n_chunks):
        rows = slice(c * GM_CHUNK, (c + 1) * GM_CHUNK)
        for g in range(GM_GROUPS):
            cols = slice(g * GM_CH, (g + 1) * GM_CH)
            mixed = _dot(w_scr[g], vn[rows, cols]) + bias_ref[:, cols]
            o_ref[rows, cols] = (_gelu(u_ref[rows, cols]) * mixed).astype(BF16)


def _gmlp(z, ln_g, ln_b, w_s, b_s, tg=512):
    T = z.shape[0]
    W = GM_GROUPS * GM_CH
    u_col = 4 * HG_HEADS * HG_D // W
    bias = jnp.repeat(b_s.T, GM_CH, axis=1)
    return pl.pallas_call(
        _gmlp_body,
        grid=(T // tg,),
        in_specs=[pl.BlockSpec((tg, W), lambda i: (i, u_col)),
                  pl.BlockSpec((tg, W), lambda i: (i, u_col + 1)),
                  pl.BlockSpec((1, W), lambda i: (0, 0)),
                  pl.BlockSpec((1, W), lambda i: (0, 0)),
                  pl.BlockSpec(w_s.shape, lambda i: (0, 0, 0)),
                  pl.BlockSpec(bias.shape, lambda i: (0, 0))],
        out_specs=pl.BlockSpec((tg, W), lambda i: (i, 0)),
        out_shape=jax.ShapeDtypeStruct((T, W), BF16),
        scratch_shapes=[pltpu.VMEM(w_s.shape, BF16)],
        compiler_params=_cparams(1),
        name="gmlp",
    )(z, z, ln_g.reshape(1, W), ln_b.reshape(1, W), w_s, bias)


def _outproj_body(o_ref, sg_ref, x_ref, wa_ref, wb_ref, g_ref, wr_ref, br_ref,
                  x1_ref, h_ref, ri_ref, rg_ref, cnt_ref, tri_scr, carry_scr):
    tm = x_ref.shape[0]

    @pl.when(pl.program_id(0) == 0)
    def _():
        r = lax.broadcasted_iota(jnp.int32, (tm, tm), 0)
        c = lax.broadcasted_iota(jnp.int32, (tm, tm), 1)
        tri_scr[...] = (c < r).astype(BF16)
        carry_scr[...] = jnp.zeros_like(carry_scr)

    x1 = x_ref[...] + _dot(o_ref[...], wa_ref[...]) + _dot(sg_ref[...], wb_ref[...])
    x1_ref[...] = x1
    h = _rms(x1, g_ref[...], RMS_EPS)
    half = h.shape[1] // 2
    h_ref[...] = _pack_bf16_pair(h[:, :half], h[:, half:])

    logits = jnp.dot(h, wr_ref[...], preferred_element_type=F32,
                     precision=lax.Precision.HIGHEST) + br_ref[...]
    lane = lax.broadcasted_iota(jnp.int32, (tm, LANES), 1)
    neg = jnp.float32(-jnp.inf)
    work = jnp.where(lane < N_EXPERTS, logits, neg)
    vals, idxs = [], []
    for _ in range(TOP_K):
        m = jnp.max(work, axis=-1, keepdims=True)
        i = jnp.min(jnp.where(work == m, lane, LANES), axis=-1, keepdims=True)
        vals.append(m)
        idxs.append(i)
        work = jnp.where(lane == i, neg, work)
    es = [jnp.exp(m - vals[0]) for m in vals]
    denom = es[0] + es[1] + es[2] + es[3]

    onehot = jnp.zeros((tm, LANES), F32)
    for i in idxs:
        onehot = onehot + (lane == i).astype(F32)
    before = _dot(tri_scr[...], onehot.astype(BF16)) + carry_scr[0:1, :]
    carry = carry_scr[0:1, :] + jnp.sum(onehot, axis=0, keepdims=True)
    carry_scr[...] = jnp.broadcast_to(carry, carry_scr.shape)
    cnt_ref[...] = jnp.broadcast_to(carry, cnt_ref.shape).astype(jnp.int32)

    ri = jnp.zeros((tm, LANES), jnp.int32)
    rg = jnp.zeros((tm, LANES), F32)
    for kk in range(TOP_K):
        rank = jnp.sum(jnp.where(lane == idxs[kk], before, 0.0), axis=-1, keepdims=True)
        ri = jnp.where(lane == kk, idxs[kk], ri)
        ri = jnp.where(lane == TOP_K + kk, rank.astype(jnp.int32), ri)
        rg = jnp.where(lane == kk, es[kk] / denom, rg)
    ri_ref[...] = ri
    rg_ref[...] = rg


def _outproj(o, sg, x2d, w_out, g, w_router, b_router, tm=512):
    T, D = x2d.shape
    Wh = o.shape[1]
    wa = w_out[:Wh].astype(BF16)
    wb = w_out[Wh:].astype(BF16)
    wr = jnp.zeros((D, LANES), F32).at[:, :N_EXPERTS].set(w_router)
    br = jnp.zeros((1, LANES), F32).at[0, :N_EXPERTS].set(b_router)
    row = lambda i: (i, 0)
    fixed = lambda i: (0, 0)
    return pl.pallas_call(
        _outproj_body,
        grid=(T // tm,),
        in_specs=[pl.BlockSpec((tm, Wh), row), pl.BlockSpec((tm, sg.shape[1]), row),
                  pl.BlockSpec((tm, D), row),
                  pl.BlockSpec(wa.shape, fixed), pl.BlockSpec(wb.shape, fixed),
                  pl.BlockSpec((1, D), fixed), pl.BlockSpec(wr.shape, fixed),
                  pl.BlockSpec(br.shape, fixed)],
        out_specs=[pl.BlockSpec((tm, D), row), pl.BlockSpec((tm, D // 2), row),
                   pl.BlockSpec((tm, LANES), row), pl.BlockSpec((tm, LANES), row),
                   pl.BlockSpec((8, LANES), fixed)],
        out_shape=[jax.ShapeDtypeStruct((T, D), F32), jax.ShapeDtypeStruct((T, D // 2), jnp.uint32),
                   jax.ShapeDtypeStruct((T, LANES), jnp.int32),
                   jax.ShapeDtypeStruct((T, LANES), F32),
                   jax.ShapeDtypeStruct((8, LANES), jnp.int32)],
        scratch_shapes=[pltpu.VMEM((tm, tm), BF16), pltpu.VMEM((8, LANES), F32)],
        compiler_params=_cparams(1),
        name="outproj_router",
    )(o, sg, x2d, wa, wb, g.reshape(1, D), wr, br)


def _dispatch_body(dest_ref, ends_ref, h_ref, xs_ref, zero_scr, sem, zsem):
    tm = h_ref.shape[0]
    base = pl.program_id(0) * (tm * TOP_K)

    @pl.when(pl.program_id(0) == 0)
    def _():
        zero_scr[...] = jnp.zeros_like(zero_scr)

        def last_block(e, op):
            end = ends_ref[e]
            prev = ends_ref[e - 1] if e else 0

            @pl.when(end > prev)
            def _():
                start = pl.multiple_of(end - EXPERT_BLOCK, EXPERT_BLOCK)
                op(pltpu.make_async_copy(zero_scr, xs_ref.at[pl.ds(start, EXPERT_BLOCK)], zsem))

        for e in range(N_EXPERTS):
            last_block(e, lambda cp: cp.start())
        for e in range(N_EXPERTS):
            last_block(e, lambda cp: cp.wait())

        def tail_copy(b):
            start = pl.multiple_of(b * EXPERT_BLOCK, EXPERT_BLOCK)
            return pltpu.make_async_copy(zero_scr, xs_ref.at[pl.ds(start, EXPERT_BLOCK)], zsem)

        first = ends_ref[N_EXPERTS - 1] // EXPERT_BLOCK
        n_blocks = xs_ref.shape[0] // EXPERT_BLOCK
        lax.fori_loop(first, n_blocks, lambda b, c: (tail_copy(b).start(), c)[1], 0)
        lax.fori_loop(first, n_blocks, lambda b, c: (tail_copy(b).wait(), c)[1], 0)

    def row_copy(r, d):
        return pltpu.make_async_copy(h_ref.at[pl.ds(r, 1)], xs_ref.at[pl.ds(d, 1)], sem)

    def issue(r, carry):
        for kk in range(TOP_K):
            row_copy(r, dest_ref[base + r * TOP_K + kk]).start()
        return carry

    lax.fori_loop(0, tm, issue, 0)

    def drain(r, carry):
        for kk in range(TOP_K):
            row_copy(0, 0).wait()
        return carry

    lax.fori_loop(0, tm, drain, 0)


def _dispatch(dest_flat, pad_ends, h, n_rows, tm=256):
    T, W = h.shape
    return pl.pallas_call(
        _dispatch_body,
        grid_spec=pltpu.PrefetchScalarGridSpec(
            num_scalar_prefetch=2,
            grid=(T // tm,),
            in_specs=[pl.BlockSpec((tm, W), lambda i, dest, ends: (i, 0))],
            out_specs=pl.BlockSpec(memory_space=pl.ANY),
            scratch_shapes=[pltpu.VMEM((EXPERT_BLOCK, W), h.dtype),
                            pltpu.SemaphoreType.DMA(()), pltpu.SemaphoreType.DMA(())]),
        out_shape=jax.ShapeDtypeStruct((n_rows, W), h.dtype),
        compiler_params=_cparams(1),
        name="dispatch",
    )(dest_flat, pad_ends, h)


def _swiglu_deinterleaved(gu, fc):
    even = lax.broadcasted_iota(jnp.int32, (gu.shape[0], LANES), 1) % 2 == 0
    acts = []
    for c in range(fc // LANES):
        a = gu[:, c * LANES:(c + 1) * LANES]
        b = gu[:, fc + c * LANES:fc + (c + 1) * LANES]
        gate = jnp.where(even, a, pltpu.roll(b, 1, axis=1))
        up = jnp.where(even, pltpu.roll(a, LANES - 1, axis=1), b)
        gate = jnp.minimum(gate, SWIGLU_LIMIT)
        up = jnp.clip(up, -SWIGLU_LIMIT, SWIGLU_LIMIT)
        acts.append(((up + 1.0) * gate * jax.nn.sigmoid(SWIGLU_ALPHA * gate)).astype(BF16))
    return jnp.concatenate(acts, axis=1)


def _interleave_rows_bf16(a, b):
    return pltpu.bitcast(_pack_bf16_pair(a, b), BF16)


def _expert_body(ie_ref, ir_ref, in_ref, nu_ref,
                 xs_ref, wgu_ref, wdn_ref, bgu_ref, bdn_ref, ys_ref,
                 xres, act, wgu_bf, wdn_bf, ystage, sem_x, sem_y, *, n_gu, n_dn):
    del ie_ref
    i = pl.program_id(0)
    j = pl.program_id(1)
    nrb = in_ref[i]
    row0 = ir_ref[i]
    fc = act.shape[2]
    nc = ystage.shape[2]
    blk = EXPERT_BLOCK

    def rows(rb, first=0):
        return pl.ds(pl.multiple_of(first + rb * blk, blk), blk)

    @pl.when((i == 0) & (j == 0))
    def _():
        ystage[0] = jnp.zeros(ystage.shape[1:], F32)
        n_blocks = ys_ref.shape[0] // blk

        def tail_copy(b, cc):
            return pltpu.make_async_copy(
                ystage.at[0, pl.ds(0, blk)], ys_ref.at[rows(b), pl.ds(cc * nc, nc)], sem_y.at[0])

        def tail_start(b, c):
            for cc in range(n_dn):
                tail_copy(b, cc).start()
            return c

        def tail_wait(b, c):
            for cc in range(n_dn):
                tail_copy(b, cc).wait()
            return c

        lax.fori_loop(nu_ref[0], n_blocks, tail_start, 0)
        lax.fori_loop(nu_ref[0], n_blocks, tail_wait, 0)

    def x_copy(rb):
        return pltpu.make_async_copy(xs_ref.at[rows(rb, row0)], xres.at[rows(rb)], sem_x)

    def for_blocks(fn):
        lax.fori_loop(0, nrb, lambda rb, c: (fn(rb), c)[1], 0)

    n_pairs = nrb // 2
    odd = nrb % 2

    def unit_rows(u, size, first=0):
        return pl.ds(pl.multiple_of(first + u * (2 * blk), blk), size)

    def for_units(fn):
        lax.fori_loop(0, n_pairs, lambda u, c: (fn(u, 2 * blk), c)[1], 0)

        @pl.when(odd == 1)
        def _():
            fn(n_pairs, blk)

    @pl.when((nrb > 0) & (j < n_gu))
    def _():
        @pl.when(j == 0)
        def _():
            for_blocks(lambda rb: x_copy(rb).start())

        wgu_bf[...] = wgu_ref[...].astype(BF16)

        @pl.when(j == 0)
        def _():
            for_blocks(lambda rb: x_copy(rb).wait())

        def unit(u, size):
            x = jnp.concatenate(_unpack_bf16_pair(xres[unit_rows(u, size), :]), axis=1)
            gu = _dot(x, wgu_bf[...]) + bgu_ref[...]
            act[j, unit_rows(u, size), :] = _swiglu_deinterleaved(gu, fc)

        for_units(unit)

    @pl.when((nrb > 0) & (j >= n_gu))
    def _():
        jd = j - n_gu
        n_slots = ystage.shape[0]
        for jf in range(n_gu):
            for c in range(fc // LANES):
                a0 = jf * fc + c * (LANES // 2)
                b0 = a0 + fc // 2
                wdn_bf[jf * fc + c * LANES:jf * fc + (c + 1) * LANES, :] = _interleave_rows_bf16(
                    wdn_ref[a0:a0 + LANES // 2, :], wdn_ref[b0:b0 + LANES // 2, :])

        def y_copy(u, size):
            slot = u % n_slots
            return pltpu.make_async_copy(
                ystage.at[slot, pl.ds(0, size)],
                ys_ref.at[unit_rows(u, size, row0), pl.ds(pl.multiple_of(jd * nc, nc), nc)],
                sem_y.at[slot])

        def unit(u, size):
            @pl.when(u >= n_slots)
            def _():
                y_copy(u - n_slots, 2 * blk).wait()

            a = jnp.concatenate([act[jf, unit_rows(u, size), :] for jf in range(n_gu)], axis=1)
            ystage[u % n_slots, 0:size, :] = _dot(a, wdn_bf[...]) + bdn_ref[...]
            y_copy(u, size).start()

        for_units(unit)
        n_units = n_pairs + odd
        for back in range(1, n_slots + 1):
            @pl.when((n_units >= back) & ((odd == 0) | (back > 1)))
            def _():
                y_copy(n_units - back, 2 * blk).wait()

        @pl.when(odd == 1)
        def _():
            y_copy(n_units - 1, blk).wait()


def _experts(items, xs, w_gate_up, w_down, b_gate_up, b_down, fc, nc):
    item_e, item_row0, item_nrb, n_used = items
    n_rows = xs.shape[0]
    E, D, F2 = w_gate_up.shape
    F = F2 // 2
    n_gu = F // fc
    n_dn = D // nc
    n_items = item_e.shape[0]

    def gu_chunk(i, j, ie, ir, inr, nu):
        return jnp.where(inr[i] > 0, jnp.minimum(j, n_gu - 1), n_gu - 1)

    def dn_chunk(i, j, ie, ir, inr, nu):
        return jnp.where(inr[i] > 0, jnp.maximum(j - n_gu, 0), n_dn - 1)

    return pl.pallas_call(
        functools.partial(_expert_body, n_gu=n_gu, n_dn=n_dn),
        grid_spec=pltpu.PrefetchScalarGridSpec(
            num_scalar_prefetch=4,
            grid=(n_items, n_gu + n_dn),
            in_specs=[
                pl.BlockSpec(memory_space=pl.ANY),
                pl.BlockSpec((None, D, 2 * fc), lambda i, j, ie, *a: (ie[i], 0, gu_chunk(i, j, ie, *a))),
                pl.BlockSpec((None, F, nc), lambda i, j, ie, *a: (ie[i], 0, dn_chunk(i, j, ie, *a))),
                pl.BlockSpec((None, 1, 2 * fc), lambda i, j, ie, *a: (ie[i], 0, gu_chunk(i, j, ie, *a))),
                pl.BlockSpec((None, 1, nc), lambda i, j, ie, *a: (ie[i], 0, dn_chunk(i, j, ie, *a))),
            ],
            out_specs=pl.BlockSpec(memory_space=pl.ANY),
            scratch_shapes=[
                pltpu.VMEM((EXPERT_ROWS, D // 2), jnp.uint32),
                pltpu.VMEM((n_gu, EXPERT_ROWS, fc), BF16),
                pltpu.VMEM((D, 2 * fc), BF16), pltpu.VMEM((F, nc), BF16),
                pltpu.VMEM((EXPERT_YSLOTS, 2 * EXPERT_BLOCK, nc), F32),
                pltpu.SemaphoreType.DMA(()), pltpu.SemaphoreType.DMA((EXPERT_YSLOTS,))]),
        out_shape=jax.ShapeDtypeStruct((n_rows, D), F32),
        compiler_params=_cparams(2),
        name="experts",
    )(item_e, item_row0, item_nrb, n_used, xs, w_gate_up, w_down, b_gate_up[:, None, :],
      b_down[:, None, :])


def _combine_body(dest_ref, x1_ref, rg_ref, p_ref, ys_ref, gp_ref, wpg_ref, wpp_ref, gf_ref,
                  out_ref, buf, sem):
    tm = x1_ref.shape[0]
    i = pl.program_id(0)
    slot = i % 2

    def row_copy(s, r, kk, d):
        return pltpu.make_async_copy(ys_ref.at[pl.ds(d, 1)], buf.at[s, kk, pl.ds(r, 1)], sem.at[s])

    def gather(step, s):
        base = step * (tm * TOP_K)

        def issue(r, carry):
            for kk in range(TOP_K):
                row_copy(s, r, kk, dest_ref[base + r * TOP_K + kk]).start()
            return carry

        lax.fori_loop(0, tm, issue, 0)

    @pl.when(i == 0)
    def _():
        gather(0, 0)

    @pl.when(i + 1 < pl.num_programs(0))
    def _():
        gather(i + 1, 1 - slot)

    def drain(r, carry):
        for kk in range(TOP_K):
            row_copy(slot, 0, kk, 0).wait()
        return carry

    lax.fori_loop(0, tm, drain, 0)

    gates = rg_ref[...]
    x2 = x1_ref[...]
    for kk in range(TOP_K):
        x2 = x2 + gates[:, kk:kk + 1] * buf[slot, kk]
    hp = _rms(x2, gp_ref[...], RMS_EPS).astype(BF16)
    gate = jax.nn.sigmoid(_dot(hp, wpg_ref[...]))
    proj = _dot(p_ref[...].astype(BF16), wpp_ref[...])
    x3 = x2 + gate * proj
    out_ref[...] = _rms(x3, gf_ref[...], RMS_EPS)


def _combine(dest_flat, x1, rg, p2d, ys, g_ple, w_gate, w_proj, g_final, tm=256):
    T, D = x1.shape
    P = p2d.shape[1]
    row = lambda i, dest: (i, 0)
    fixed = lambda i, dest: (0, 0)
    return pl.pallas_call(
        _combine_body,
        grid_spec=pltpu.PrefetchScalarGridSpec(
            num_scalar_prefetch=1,
            grid=(T // tm,),
            in_specs=[pl.BlockSpec((tm, D), row), pl.BlockSpec((tm, LANES), row),
                      pl.BlockSpec((tm, P), row), pl.BlockSpec(memory_space=pl.ANY),
                      pl.BlockSpec((1, D), fixed), pl.BlockSpec((D, D), fixed),
                      pl.BlockSpec((P, D), fixed), pl.BlockSpec((1, D), fixed)],
            out_specs=pl.BlockSpec((tm, D), row),
            scratch_shapes=[pltpu.VMEM((2, TOP_K, tm, D), F32), pltpu.SemaphoreType.DMA((2,))]),
        out_shape=jax.ShapeDtypeStruct((T, D), F32),
        compiler_params=_cparams(1),
        name="combine_ple",
    )(dest_flat, x1, rg, p2d, ys, g_ple.reshape(1, D), w_gate.astype(BF16),
      w_proj.astype(BF16), g_final.reshape(1, D))


def _routing_tables(ri, counts_block, n_items):
    counts = counts_block[0, :N_EXPERTS]
    padded = (counts + EXPERT_BLOCK - 1) // EXPERT_BLOCK * EXPERT_BLOCK
    pad_ends = jnp.cumsum(padded)
    pad_starts = pad_ends - padded
    idx = ri[:, :TOP_K]
    rank = ri[:, TOP_K:2 * TOP_K]
    dest = (pad_starts[idx] + rank).reshape(-1).astype(jnp.int32)

    per_expert = (padded + EXPERT_ROWS - 1) // EXPERT_ROWS
    item_ends = jnp.cumsum(per_expert)
    total = item_ends[-1]
    ids = jnp.arange(n_items, dtype=jnp.int32)
    e_of = jnp.sum(jnp.minimum(ids, total - 1)[:, None] >= item_ends[None, :], axis=1)
    e_of = jnp.clip(e_of, 0, N_EXPERTS - 1).astype(jnp.int32)
    local = ids - (item_ends - per_expert)[e_of]
    active = ids < total
    row0 = jnp.where(active, pad_starts[e_of] + local * EXPERT_ROWS, 0)
    left = jnp.clip(padded[e_of] - local * EXPERT_ROWS, 0, EXPERT_ROWS)
    nrb = jnp.where(active, left // EXPERT_BLOCK, 0)
    n_used = pad_ends[-1:] // EXPERT_BLOCK
    i32 = lambda a: a.astype(jnp.int32)
    return dest, i32(pad_ends), (e_of, i32(row0), i32(nrb), i32(n_used))


def _layer(x2d, p2d, B, S, norm_mix, w_in, lb, hgrn_out_norm, gmlp_ln_g, gmlp_ln_b, w_spatial,
           b_spatial, w_out, norm_ffn, w_router, b_router, w_gate_up, b_gate_up, w_down, b_down,
           norm_ple, w_ple_gate, w_ple_proj, g_out):
    T = x2d.shape[0]
    z = _inproj(x2d, norm_mix, w_in.astype(BF16))
    o = _hgrn(z, lb, hgrn_out_norm, B, S)
    sg = _gmlp(z, gmlp_ln_g, gmlp_ln_b, w_spatial, b_spatial)
    x1, h2, ri, rg, counts = _outproj(o, sg, x2d, w_out, norm_ffn, w_router, b_router)

    n_rows = T * TOP_K + N_EXPERTS * EXPERT_BLOCK
    n_items = N_EXPERTS + T * TOP_K // EXPERT_ROWS
    dest, pad_ends, items = _routing_tables(ri, counts, n_items)
    xs = _dispatch(dest, pad_ends, h2, n_rows)
    ys = _experts(items, xs, w_gate_up, w_down, b_gate_up, b_down, EXPERT_FC, EXPERT_NC)
    return _combine(dest, x1, rg, p2d, ys, norm_ple, w_ple_gate, w_ple_proj, g_out)


def kernel(x, p, norm_mix, w_in, lb_logits, hgrn_out_norm, gmlp_ln_g, gmlp_ln_b, w_spatial, b_spatial, w_out, norm_ffn, w_router, b_router, w_gate_up, b_gate_up, w_down, b_down, norm_ple, w_ple_gate, w_ple_proj, norm_final):
    B, S, D = x.shape
    depth = p.shape[0]
    assert depth == 1, "the final norm is fused into the last layer's combine kernel"
    lower_bounds = jnp.cumsum(jax.nn.softmax(lb_logits.astype(F32), axis=0), axis=0)
    out = _layer(x.reshape(B * S, D), p[0].reshape(B * S, -1), B, S, norm_mix[0], w_in[0],
                 lower_bounds[0], hgrn_out_norm[0], gmlp_ln_g[0], gmlp_ln_b[0], w_spatial[0],
                 b_spatial[0], w_out[0], norm_ffn[0], w_router[0], b_router[0], w_gate_up[0],
                 b_gate_up[0], w_down[0], b_down[0], norm_ple[0], w_ple_gate[0], w_ple_proj[0],
                 norm_final)
    return out.reshape(B, S, D)
```

```python
import functools
import math

import numpy as np
import jax
import jax.numpy as jnp
from jax import lax
from jax.experimental import pallas as pl
from jax.experimental.pallas import tpu as pltpu

F32 = jnp.float32
BF16 = jnp.bfloat16

HG_HEADS = 8
HG_D = 128
GM_GROUPS = 8
GM_CH = 128
GM_CHUNK = 128
N_EXPERTS = 32
TOP_K = 4
SWIGLU_LIMIT = 7.0
SWIGLU_ALPHA = 1.702
RMS_EPS = 1e-6
LN_EPS = 1e-5

LANES = 128
HGRN_TILE = 256
HGRN_HEADS_PER_STEP = 2
EXPERT_BLOCK = 256
EXPERT_ROWS = 2560
EXPERT_FC = 512
EXPERT_NC = 512
EXPERT_YSLOTS = 2
VMEM_LIMIT = 56 * 1024 * 1024


def _cparams(n_axes):
    return pltpu.CompilerParams(dimension_semantics=("arbitrary",) * n_axes,
                                vmem_limit_bytes=VMEM_LIMIT)


def _rms(x, g, eps):
    return x * lax.rsqrt(jnp.mean(x * x, axis=-1, keepdims=True) + eps) * g


def _dot(a, b):
    return jnp.dot(a, b, preferred_element_type=F32)


def _dot_nt(a, b):
    return lax.dot_general(a, b, (((1,), (1,)), ((), ())), preferred_element_type=F32)


def _bf16_bits(x):
    return pltpu.bitcast(x.astype(BF16).astype(F32), jnp.uint32) & jnp.uint32(0xFFFF0000)


def _pack_bf16_pair(lo, hi):
    return (_bf16_bits(lo) >> 16) | _bf16_bits(hi)


def _unpack_bf16_pair(w):
    lo = pltpu.bitcast(w << 16, F32).astype(BF16)
    hi = pltpu.bitcast(w & jnp.uint32(0xFFFF0000), F32).astype(BF16)
    return lo, hi


def _gelu(x):
    return 0.5 * x * (1.0 + lax.erf(x * (1.0 / math.sqrt(2.0))))


def _inproj_body(x_ref, g_ref, w_ref, z_ref, h_scr):
    @pl.when(pl.program_id(1) == 0)
    def _():
        h_scr[...] = _rms(x_ref[...], g_ref[...], RMS_EPS).astype(BF16)

    z_ref[...] = _dot(h_scr[...], w_ref[...])


def _inproj(x2d, g, w, tm=1024, tn=1024):
    T, D = x2d.shape
    N = w.shape[1]
    return pl.pallas_call(
        _inproj_body,
        grid=(T // tm, N // tn),
        in_specs=[pl.BlockSpec((tm, D), lambda i, j: (i, 0)),
                  pl.BlockSpec((1, D), lambda i, j: (0, 0)),
                  pl.BlockSpec((D, tn), lambda i, j: (0, j))],
        out_specs=pl.BlockSpec((tm, tn), lambda i, j: (i, j)),
        out_shape=jax.ShapeDtypeStruct((T, N), F32),
        scratch_shapes=[pltpu.VMEM((tm, D), BF16)],
        compiler_params=_cparams(2),
        name="inproj",
    )(x2d, g.reshape(1, D), w)


HGRN_MIN_BROADCAST = 8


def _hgrn_constants(C):
    nlev = int(math.log2(C))
    r = np.arange(C)
    mats = [(r[None, :] <= r[:, None])]
    level = np.full((C, C), -1, np.int32)
    level[r, r] = 0
    for l in range(1, nlev + 1):
        L = C >> l
        pos = r % (2 * L)
        mid = (r // (2 * L)) * (2 * L) + L - 1
        second = pos >= L
        if L < HGRN_MIN_BROADCAST:
            mats.append(np.where(second[:, None],
                                 (r[None, :] > mid[:, None]) & (r[None, :] <= r[:, None]),
                                 (r[None, :] > r[:, None]) & (r[None, :] <= mid[:, None])))
        same = (r[:, None] // (2 * L)) == (r[None, :] // (2 * L))
        level[same & second[:, None] & (~second)[None, :]] = l
    h = C // 2
    assert np.array_equal(level[:h, :h], level[h:, h:])
    w = np.concatenate(mats, axis=0).astype(np.float32)
    return jnp.asarray(w, BF16), jnp.asarray(level[:h, :h]), nlev


def _hgrn_head(q, f_logit, v, out_gate, lb, gn, st, w_ref, lv, nlev):
    C = q.shape[0]
    h = C // 2
    q = q * jax.nn.sigmoid(q)
    f = lb + (1.0 - lb) * jax.nn.sigmoid(f_logit)
    g = jnp.log(f)
    k = 1.0 - f
    vb = v.astype(BF16)

    g1 = g.astype(BF16)
    r1 = g - g1.astype(F32)
    g2 = r1.astype(BF16)
    g3 = (r1 - g2.astype(F32)).astype(BF16)
    gcat = jnp.concatenate([g1, g2, g3], axis=1)

    def range_sum(i):
        e3 = _dot(w_ref[i * C:(i + 1) * C, :], gcat)
        return e3[:, :HG_D] + e3[:, HG_D:2 * HG_D] + e3[:, 2 * HG_D:]

    b = range_sum(0)
    b_last = b[C - 1:C, :]

    def level_exponent(l):
        L = C >> l
        if L < HGRN_MIN_BROADCAST:
            return range_sum(1 + (l - (nlev + 1 - int(math.log2(HGRN_MIN_BROADCAST)))))
        mids = [jnp.broadcast_to(b[s + L - 1:s + L, :], (2 * L, HG_D)) for s in range(0, C, 2 * L)]
        mid = mids[0] if len(mids) == 1 else jnp.concatenate(mids, axis=0)
        return -jnp.abs(b - mid)

    def scaled(l):
        e = jnp.exp(level_exponent(l))
        return (q * e).astype(BF16), (k * e).astype(BF16)

    ql, kl = scaled(1)
    a_cross = _dot_nt(ql[h:], kl[:h])
    qb, kb = q.astype(BF16), k.astype(BF16)
    a_top = jnp.where(lv == 0, _dot_nt(qb[:h], kb[:h]), 0.0)
    a_bot = jnp.where(lv == 0, _dot_nt(qb[h:], kb[h:]), 0.0)
    for l in range(2, nlev + 1):
        ql, kl = scaled(l)
        a_top = jnp.where(lv == l, _dot_nt(ql[:h], kl[:h]), a_top)
        a_bot = jnp.where(lv == l, _dot_nt(ql[h:], kl[h:]), a_bot)

    o_top = _dot(a_top.astype(BF16), vb[:h])
    o_bot = _dot(a_cross.astype(BF16), vb[:h]) + _dot(a_bot.astype(BF16), vb[h:])
    o = _dot_nt((q * jnp.exp(b)).astype(BF16), st.astype(BF16))
    o = o + jnp.concatenate([o_top, o_bot], axis=0)

    khat = (k * jnp.exp(b_last - b)).astype(BF16)
    st_new = st * jnp.exp(b_last) + _dot(v.T.astype(BF16), khat)
    o = _rms(o, gn, RMS_EPS) * jax.nn.sigmoid(out_gate)
    return o.astype(BF16), st_new


def _hgrn_body(q_ref, f_ref, v_ref, og_ref, lb_ref, gn_ref, w_ref, lv_ref, o_ref, st_scr, *, nlev):
    @pl.when(pl.program_id(2) == 0)
    def _():
        st_scr[...] = jnp.zeros_like(st_scr)

    lv = lv_ref[...]
    for hh in range(st_scr.shape[0]):
        lanes = slice(hh * HG_D, (hh + 1) * HG_D)
        o, st_new = _hgrn_head(q_ref[:, lanes], f_ref[:, lanes], v_ref[:, lanes], og_ref[:, lanes],
                               lb_ref[hh], gn_ref[hh], st_scr[hh], w_ref, lv, nlev)
        o_ref[:, lanes] = o
        st_scr[hh] = st_new


def _hgrn(z, lb, gn, B, S):
    C = HGRN_TILE
    H = HG_HEADS
    n_s = S // C
    w, lv, nlev = _hgrn_constants(C)
    hp = HGRN_HEADS_PER_STEP
    row = lambda b, h, s: b * n_s + s
    col = lambda off: pl.BlockSpec((C, hp * HG_D),
                                   lambda b, h, s: (row(b, h, s), off * (H // hp) + h))
    per_head = pl.BlockSpec((hp, 1, HG_D), lambda b, h, s: (h, 0, 0))
    return pl.pallas_call(
        functools.partial(_hgrn_body, nlev=nlev),
        grid=(B, H // hp, n_s),
        in_specs=[col(0), col(1), col(2), col(3), per_head, per_head,
                  pl.BlockSpec(w.shape, lambda b, h, s: (0, 0)),
                  pl.BlockSpec(lv.shape, lambda b, h, s: (0, 0))],
        out_specs=pl.BlockSpec((C, hp * HG_D), lambda b, h, s: (row(b, h, s), h)),
        out_shape=jax.ShapeDtypeStruct((B * S, H * HG_D), BF16),
        scratch_shapes=[pltpu.VMEM((hp, HG_D, HG_D), F32)],
        compiler_params=_cparams(3),
        name="hgrn",
    )(z, z, z, z, lb.reshape(H, 1, HG_D), gn.reshape(H, 1, HG_D), w, lv)


def _gmlp_body(u_ref, v_ref, lng_ref, lnb_ref, ws_ref, bias_ref, o_ref, w_scr):
    @pl.when(pl.program_id(0) == 0)
    def _():
        r = lax.broadcasted_iota(jnp.int32, ws_ref.shape, 1)
        c = lax.broadcasted_iota(jnp.int32, ws_ref.shape, 2)
        w_scr[...] = jnp.where(r >= c, ws_ref[...], 0.0).astype(BF16)

    v = _gelu(v_ref[...])
    mu = jnp.mean(v, axis=-1, keepdims=True)
    vc = v - mu
    vn = vc * lax.rsqrt(jnp.mean(vc * vc, axis=-1, keepdims=True) + LN_EPS)
    vn = (vn * lng_ref[...] + lnb_ref[...]).astype(BF16)
    n_chunks = u_ref.shape[0] // GM_CHUNK
    for c in range(n_chunks):
        rows = slice(c * GM_CHUNK, (c + 1) * GM_CHUNK)
        for g in range(GM_GROUPS):
            cols = slice(g * GM_CH, (g + 1) * GM_CH)
            mixed = _dot(w_scr[g], vn[rows, cols]) + bias_ref[:, cols]
            o_ref[rows, cols] = (_gelu(u_ref[rows, cols]) * mixed).astype(BF16)


def _gmlp(z, ln_g, ln_b, w_s, b_s, tg=512):
    T = z.shape[0]
    W = GM_GROUPS * GM_CH
    u_col = 4 * HG_HEADS * HG_D // W
    bias = jnp.repeat(b_s.T, GM_CH, axis=1)
    return pl.pallas_call(
        _gmlp_body,
        grid=(T // tg,),
        in_specs=[pl.BlockSpec((tg, W), lambda i: (i, u_col)),
                  pl.BlockSpec((tg, W), lambda i: (i, u_col + 1)),
                  pl.BlockSpec((1, W), lambda i: (0, 0)),
                  pl.BlockSpec((1, W), lambda i: (0, 0)),
                  pl.BlockSpec(w_s.shape, lambda i: (0, 0, 0)),
                  pl.BlockSpec(bias.shape, lambda i: (0, 0))],
        out_specs=pl.BlockSpec((tg, W), lambda i: (i, 0)),
        out_shape=jax.ShapeDtypeStruct((T, W), BF16),
        scratch_shapes=[pltpu.VMEM(w_s.shape, BF16)],
        compiler_params=_cparams(1),
        name="gmlp",
    )(z, z, ln_g.reshape(1, W), ln_b.reshape(1, W), w_s, bias)


def _outproj_body(o_ref, sg_ref, x_ref, wa_ref, wb_ref, g_ref, wr_ref, br_ref,
                  x1_ref, h_ref, ri_ref, rg_ref, cnt_ref, tri_scr, carry_scr):
    tm = x_ref.shape[0]

    @pl.when(pl.program_id(0) == 0)
    def _():
        r = lax.broadcasted_iota(jnp.int32, (tm, tm), 0)
        c = lax.broadcasted_iota(jnp.int32, (tm, tm), 1)
        tri_scr[...] = (c < r).astype(BF16)
        carry_scr[...] = jnp.zeros_like(carry_scr)

    x1 = x_ref[...] + _dot(o_ref[...], wa_ref[...]) + _dot(sg_ref[...], wb_ref[...])
    x1_ref[...] = x1
    h = _rms(x1, g_ref[...], RMS_EPS)
    half = h.shape[1] // 2
    h_ref[...] = _pack_bf16_pair(h[:, :half], h[:, half:])

    logits = jnp.dot(h, wr_ref[...], preferred_element_type=F32,
                     precision=lax.Precision.HIGHEST) + br_ref[...]
    lane = lax.broadcasted_iota(jnp.int32, (tm, LANES), 1)
    neg = jnp.float32(-jnp.inf)
    work = jnp.where(lane < N_EXPERTS, logits, neg)
    vals, idxs = [], []
    for _ in range(TOP_K):
        m = jnp.max(work, axis=-1, keepdims=True)
        i = jnp.min(jnp.where(work == m, lane, LANES), axis=-1, keepdims=True)
        vals.append(m)
        idxs.append(i)
        work = jnp.where(lane == i, neg, work)
    es = [jnp.exp(m - vals[0]) for m in vals]
    denom = es[0] + es[1] + es[2] + es[3]

    onehot = jnp.zeros((tm, LANES), F32)
    for i in idxs:
        onehot = onehot + (lane == i).astype(F32)
    before = _dot(tri_scr[...], onehot.astype(BF16)) + carry_scr[0:1, :]
    carry = carry_scr[0:1, :] + jnp.sum(onehot, axis=0, keepdims=True)
    carry_scr[...] = jnp.broadcast_to(carry, carry_scr.shape)
    cnt_ref[...] = jnp.broadcast_to(carry, cnt_ref.shape).astype(jnp.int32)

    ri = jnp.zeros((tm, LANES), jnp.int32)
    rg = jnp.zeros((tm, LANES), F32)
    for kk in range(TOP_K):
        rank = jnp.sum(jnp.where(lane == idxs[kk], before, 0.0), axis=-1, keepdims=True)
        ri = jnp.where(lane == kk, idxs[kk], ri)
        ri = jnp.where(lane == TOP_K + kk, rank.astype(jnp.int32), ri)
        rg = jnp.where(lane == kk, es[kk] / denom, rg)
    ri_ref[...] = ri
    rg_ref[...] = rg


def _outproj(o, sg, x2d, w_out, g, w_router, b_router, tm=512):
    T, D = x2d.shape
    Wh = o.shape[1]
    wa = w_out[:Wh].astype(BF16)
    wb = w_out[Wh:].astype(BF16)
    wr = jnp.zeros((D, LANES), F32).at[:, :N_EXPERTS].set(w_router)
    br = jnp.zeros((1, LANES), F32).at[0, :N_EXPERTS].set(b_router)
    row = lambda i: (i, 0)
    fixed = lambda i: (0, 0)
    return pl.pallas_call(
        _outproj_body,
        grid=(T // tm,),
        in_specs=[pl.BlockSpec((tm, Wh), row), pl.BlockSpec((tm, sg.shape[1]), row),
                  pl.BlockSpec((tm, D), row),
                  pl.BlockSpec(wa.shape, fixed), pl.BlockSpec(wb.shape, fixed),
                  pl.BlockSpec((1, D), fixed), pl.BlockSpec(wr.shape, fixed),
                  pl.BlockSpec(br.shape, fixed)],
        out_specs=[pl.BlockSpec((tm, D), row), pl.BlockSpec((tm, D // 2), row),
                   pl.BlockSpec((tm, LANES), row), pl.BlockSpec((tm, LANES), row),
                   pl.BlockSpec((8, LANES), fixed)],
        out_shape=[jax.ShapeDtypeStruct((T, D), F32), jax.ShapeDtypeStruct((T, D // 2), jnp.uint32),
                   jax.ShapeDtypeStruct((T, LANES), jnp.int32),
                   jax.ShapeDtypeStruct((T, LANES), F32),
                   jax.ShapeDtypeStruct((8, LANES), jnp.int32)],
        scratch_shapes=[pltpu.VMEM((tm, tm), BF16), pltpu.VMEM((8, LANES), F32)],
        compiler_params=_cparams(1),
        name="outproj_router",
    )(o, sg, x2d, wa, wb, g.reshape(1, D), wr, br)


def _dispatch_body(dest_ref, ends_ref, h_ref, xs_ref, zero_scr, sem, zsem):
    tm = h_ref.shape[0]
    base = pl.program_id(0) * (tm * TOP_K)

    @pl.when(pl.program_id(0) == 0)
    def _():
        zero_scr[...] = jnp.zeros_like(zero_scr)

        def last_block(e, op):
            end = ends_ref[e]
            prev = ends_ref[e - 1] if e else 0

            @pl.when(end > prev)
            def _():
                start = pl.multiple_of(end - EXPERT_BLOCK, EXPERT_BLOCK)
                op(pltpu.make_async_copy(zero_scr, xs_ref.at[pl.ds(start, EXPERT_BLOCK)], zsem))

        for e in range(N_EXPERTS):
            last_block(e, lambda cp: cp.start())
        for e in range(N_EXPERTS):
            last_block(e, lambda cp: cp.wait())

        def tail_copy(b):
            start = pl.multiple_of(b * EXPERT_BLOCK, EXPERT_BLOCK)
            return pltpu.make_async_copy(zero_scr, xs_ref.at[pl.ds(start, EXPERT_BLOCK)], zsem)

        first = ends_ref[N_EXPERTS - 1] // EXPERT_BLOCK
        n_blocks = xs_ref.shape[0] // EXPERT_BLOCK
        lax.fori_loop(first, n_blocks, lambda b, c: (tail_copy(b).start(), c)[1], 0)
        lax.fori_loop(first, n_blocks, lambda b, c: (tail_copy(b).wait(), c)[1], 0)

    def row_copy(r, d):
        return pltpu.make_async_copy(h_ref.at[pl.ds(r, 1)], xs_ref.at[pl.ds(d, 1)], sem)

    def issue(r, carry):
        for kk in range(TOP_K):
            row_copy(r, dest_ref[base + r * TOP_K + kk]).start()
        return carry

    lax.fori_loop(0, tm, issue, 0)

    def drain(r, carry):
        for kk in range(TOP_K):
            row_copy(0, 0).wait()
        return carry

    lax.fori_loop(0, tm, drain, 0)


def _dispatch(dest_flat, pad_ends, h, n_rows, tm=256):
    T, W = h.shape
    return pl.pallas_call(
        _dispatch_body,
        grid_spec=pltpu.PrefetchScalarGridSpec(
            num_scalar_prefetch=2,
            grid=(T // tm,),
            in_specs=[pl.BlockSpec((tm, W), lambda i, dest, ends: (i, 0))],
            out_specs=pl.BlockSpec(memory_space=pl.ANY),
            scratch_shapes=[pltpu.VMEM((EXPERT_BLOCK, W), h.dtype),
                            pltpu.SemaphoreType.DMA(()), pltpu.SemaphoreType.DMA(())]),
        out_shape=jax.ShapeDtypeStruct((n_rows, W), h.dtype),
        compiler_params=_cparams(1),
        name="dispatch",
    )(dest_flat, pad_ends, h)


def _swiglu_deinterleaved(gu, fc):
    even = lax.broadcasted_iota(jnp.int32, (gu.shape[0], LANES), 1) % 2 == 0
    acts = []
    for c in range(fc // LANES):
        a = gu[:, c * LANES:(c + 1) * LANES]
        b = gu[:, fc + c * LANES:fc + (c + 1) * LANES]
        gate = jnp.where(even, a, pltpu.roll(b, 1, axis=1))
        up = jnp.where(even, pltpu.roll(a, LANES - 1, axis=1), b)
        gate = jnp.minimum(gate, SWIGLU_LIMIT)
        up = jnp.clip(up, -SWIGLU_LIMIT, SWIGLU_LIMIT)
        acts.append(((up + 1.0) * gate * jax.nn.sigmoid(SWIGLU_ALPHA * gate)).astype(BF16))
    return jnp.concatenate(acts, axis=1)


def _interleave_rows_bf16(a, b):
    return pltpu.bitcast(_pack_bf16_pair(a, b), BF16)


def _expert_body(ie_ref, ir_ref, in_ref, nu_ref,
                 xs_ref, wgu_ref, wdn_ref, bgu_ref, bdn_ref, ys_ref,
                 xres, act, wgu_bf, wdn_bf, ystage, sem_x, sem_y, *, n_gu, n_dn):
    del ie_ref
    i = pl.program_id(0)
    j = pl.program_id(1)
    nrb = in_ref[i]
    row0 = ir_ref[i]
    fc = act.shape[2]
    nc = ystage.shape[2]
    blk = EXPERT_BLOCK

    def rows(rb, first=0):
        return pl.ds(pl.multiple_of(first + rb * blk, blk), blk)

    @pl.when((i == 0) & (j == 0))
    def _():
        ystage[0] = jnp.zeros(ystage.shape[1:], ystage.dtype)
        n_blocks = ys_ref.shape[0] // blk

        def tail_copy(b, cc):
            return pltpu.make_async_copy(
                ystage.at[0, pl.ds(0, blk)], ys_ref.at[rows(b), pl.ds(cc * nc, nc)], sem_y.at[0])

        def tail_start(b, c):
            for cc in range(n_dn):
                tail_copy(b, cc).start()
            return c

        def tail_wait(b, c):
            for cc in range(n_dn):
                tail_copy(b, cc).wait()
            return c

        lax.fori_loop(nu_ref[0], n_blocks, tail_start, 0)
        lax.fori_loop(nu_ref[0], n_blocks, tail_wait, 0)

    def x_copy(rb):
        return pltpu.make_async_copy(xs_ref.at[rows(rb, row0)], xres.at[rows(rb)], sem_x)

    def for_blocks(fn):
        lax.fori_loop(0, nrb, lambda rb, c: (fn(rb), c)[1], 0)

    n_pairs = nrb // 2
    odd = nrb % 2

    def unit_rows(u, size, first=0):
        return pl.ds(pl.multiple_of(first + u * (2 * blk), blk), size)

    def for_units(fn):
        lax.fori_loop(0, n_pairs, lambda u, c: (fn(u, 2 * blk), c)[1], 0)

        @pl.when(odd == 1)
        def _():
            fn(n_pairs, blk)

    @pl.when((nrb > 0) & (j < n_gu))
    def _():
        @pl.when(j == 0)
        def _():
            for_blocks(lambda rb: x_copy(rb).start())

        wgu_bf[...] = wgu_ref[...].astype(BF16)

        @pl.when(j == 0)
        def _():
            for_blocks(lambda rb: x_copy(rb).wait())

        def unit(u, size):
            x = jnp.concatenate(_unpack_bf16_pair(xres[unit_rows(u, size), :]), axis=1)
            gu = _dot(x, wgu_bf[...]) + bgu_ref[...]
            act[j, unit_rows(u, size), :] = _swiglu_deinterleaved(gu, fc)

        for_units(unit)

    @pl.when((nrb > 0) & (j >= n_gu))
    def _():
        jd = j - n_gu
        n_slots = ystage.shape[0]
        for jf in range(n_gu):
            for c in range(fc // LANES):
                a0 = jf * fc + c * (LANES // 2)
                b0 = a0 + fc // 2
                wdn_bf[jf * fc + c * LANES:jf * fc + (c + 1) * LANES, :] = _interleave_rows_bf16(
                    wdn_ref[a0:a0 + LANES // 2, :], wdn_ref[b0:b0 + LANES // 2, :])

        def y_copy(u, size):
            slot = u % n_slots
            return pltpu.make_async_copy(
                ystage.at[slot, pl.ds(0, size)],
                ys_ref.at[unit_rows(u, size, row0), pl.ds(pl.multiple_of(jd * nc, nc), nc)],
                sem_y.at[slot])

        def unit(u, size):
            @pl.when(u >= n_slots)
            def _():
                y_copy(u - n_slots, 2 * blk).wait()

            a = jnp.concatenate([act[jf, unit_rows(u, size), :] for jf in range(n_gu)], axis=1)
            y = _dot(a, wdn_bf[...]) + bdn_ref[...]
            ystage[u % n_slots, 0:size, :] = _pack_bf16_pair(y[:, :nc], y[:, nc:])
            y_copy(u, size).start()

        for_units(unit)
        n_units = n_pairs + odd
        for back in range(1, n_slots + 1):
            @pl.when((n_units >= back) & ((odd == 0) | (back > 1)))
            def _():
                y_copy(n_units - back, 2 * blk).wait()

        @pl.when(odd == 1)
        def _():
            y_copy(n_units - 1, blk).wait()


def _experts(items, xs, w_gate_up, w_down, b_gate_up, b_down, fc, nc):
    item_e, item_row0, item_nrb, n_used = items
    n_rows = xs.shape[0]
    E, D, F2 = w_gate_up.shape
    F = F2 // 2
    n_gu = F // fc
    n_dn = D // nc
    n_items = item_e.shape[0]

    def gu_chunk(i, j, ie, ir, inr, nu):
        return jnp.where(inr[i] > 0, jnp.minimum(j, n_gu - 1), n_gu - 1)

    def dn_chunk(i, j, ie, ir, inr, nu):
        return jnp.where(inr[i] > 0, jnp.maximum(j - n_gu, 0), n_dn - 1)

    return pl.pallas_call(
        functools.partial(_expert_body, n_gu=n_gu, n_dn=n_dn),
        grid_spec=pltpu.PrefetchScalarGridSpec(
            num_scalar_prefetch=4,
            grid=(n_items, n_gu + n_dn),
            in_specs=[
                pl.BlockSpec(memory_space=pl.ANY),
                pl.BlockSpec((None, D, 2 * fc), lambda i, j, ie, *a: (ie[i], 0, gu_chunk(i, j, ie, *a))),
                pl.BlockSpec((None, F, nc), lambda i, j, ie, *a: (ie[i], 0, dn_chunk(i, j, ie, *a))),
                pl.BlockSpec((None, 1, 2 * fc), lambda i, j, ie, *a: (ie[i], 0, gu_chunk(i, j, ie, *a))),
                pl.BlockSpec((None, 1, nc), lambda i, j, ie, *a: (ie[i], 0, dn_chunk(i, j, ie, *a))),
            ],
            out_specs=pl.BlockSpec(memory_space=pl.ANY),
            scratch_shapes=[
                pltpu.VMEM((EXPERT_ROWS, D // 2), jnp.uint32),
                pltpu.VMEM((n_gu, EXPERT_ROWS, fc), BF16),
                pltpu.VMEM((D, 2 * fc), BF16), pltpu.VMEM((F, nc), BF16),
                pltpu.VMEM((EXPERT_YSLOTS, 2 * EXPERT_BLOCK, nc // 2), jnp.uint32),
                pltpu.SemaphoreType.DMA(()), pltpu.SemaphoreType.DMA((EXPERT_YSLOTS,))]),
        out_shape=jax.ShapeDtypeStruct((n_rows, D // 2), jnp.uint32),
        compiler_params=_cparams(2),
        name="experts",
    )(item_e, item_row0, item_nrb, n_used, xs, w_gate_up, w_down, b_gate_up[:, None, :],
      b_down[:, None, :])


def _combine_body(dest_ref, x1_ref, rg_ref, p_ref, ys_ref, gp_ref, wpg_ref, wpp_ref, gf_ref,
                  out_ref, buf, sem):
    tm = x1_ref.shape[0]
    i = pl.program_id(0)
    slot = i % 2

    def row_copy(s, r, kk, d):
        return pltpu.make_async_copy(ys_ref.at[pl.ds(d, 1)], buf.at[s, kk, pl.ds(r, 1)], sem.at[s])

    def gather(step, s):
        base = step * (tm * TOP_K)

        def issue(r, carry):
            for kk in range(TOP_K):
                row_copy(s, r, kk, dest_ref[base + r * TOP_K + kk]).start()
            return carry

        lax.fori_loop(0, tm, issue, 0)

    @pl.when(i == 0)
    def _():
        gather(0, 0)

    @pl.when(i + 1 < pl.num_programs(0))
    def _():
        gather(i + 1, 1 - slot)

    def drain(r, carry):
        for kk in range(TOP_K):
            row_copy(slot, 0, kk, 0).wait()
        return carry

    lax.fori_loop(0, tm, drain, 0)

    def unpacked(w):
        pc = EXPERT_NC // 2
        parts = []
        for c in range(w.shape[1] // pc):
            chunk = w[:, c * pc:(c + 1) * pc]
            parts.append(pltpu.bitcast(chunk << 16, F32))
            parts.append(pltpu.bitcast(chunk & jnp.uint32(0xFFFF0000), F32))
        return jnp.concatenate(parts, axis=1)

    gates = rg_ref[...]
    x2 = x1_ref[...]
    for kk in range(TOP_K):
        x2 = x2 + gates[:, kk:kk + 1] * unpacked(buf[slot, kk])
    hp = _rms(x2, gp_ref[...], RMS_EPS).astype(BF16)
    gate = jax.nn.sigmoid(_dot(hp, wpg_ref[...]))
    proj = _dot(p_ref[...].astype(BF16), wpp_ref[...])
    x3 = x2 + gate * proj
    out_ref[...] = _rms(x3, gf_ref[...], RMS_EPS)


def _combine(dest_flat, x1, rg, p2d, ys, g_ple, w_gate, w_proj, g_final, tm=256):
    T, D = x1.shape
    P = p2d.shape[1]
    row = lambda i, dest: (i, 0)
    fixed = lambda i, dest: (0, 0)
    return pl.pallas_call(
        _combine_body,
        grid_spec=pltpu.PrefetchScalarGridSpec(
            num_scalar_prefetch=1,
            grid=(T // tm,),
            in_specs=[pl.BlockSpec((tm, D), row), pl.BlockSpec((tm, LANES), row),
                      pl.BlockSpec((tm, P), row), pl.BlockSpec(memory_space=pl.ANY),
                      pl.BlockSpec((1, D), fixed), pl.BlockSpec((D, D), fixed),
                      pl.BlockSpec((P, D), fixed), pl.BlockSpec((1, D), fixed)],
            out_specs=pl.BlockSpec((tm, D), row),
            scratch_shapes=[pltpu.VMEM((2, TOP_K, tm, D // 2), jnp.uint32),
                            pltpu.SemaphoreType.DMA((2,))]),
        out_shape=jax.ShapeDtypeStruct((T, D), F32),
        compiler_params=_cparams(1),
        name="combine_ple",
    )(dest_flat, x1, rg, p2d, ys, g_ple.reshape(1, D), w_gate.astype(BF16),
      w_proj.astype(BF16), g_final.reshape(1, D))


def _routing_tables(ri, counts_block, n_items):
    counts = counts_block[0, :N_EXPERTS]
    padded = (counts + EXPERT_BLOCK - 1) // EXPERT_BLOCK * EXPERT_BLOCK
    pad_ends = jnp.cumsum(padded)
    pad_starts = pad_ends - padded
    idx = ri[:, :TOP_K]
    rank = ri[:, TOP_K:2 * TOP_K]
    dest = (pad_starts[idx] + rank).reshape(-1).astype(jnp.int32)

    per_expert = (padded + EXPERT_ROWS - 1) // EXPERT_ROWS
    item_ends = jnp.cumsum(per_expert)
    total = item_ends[-1]
    ids = jnp.arange(n_items, dtype=jnp.int32)
    e_of = jnp.sum(jnp.minimum(ids, total - 1)[:, None] >= item_ends[None, :], axis=1)
    e_of = jnp.clip(e_of, 0, N_EXPERTS - 1).astype(jnp.int32)
    local = ids - (item_ends - per_expert)[e_of]
    active = ids < total
    row0 = jnp.where(active, pad_starts[e_of] + local * EXPERT_ROWS, 0)
    left = jnp.clip(padded[e_of] - local * EXPERT_ROWS, 0, EXPERT_ROWS)
    nrb = jnp.where(active, left // EXPERT_BLOCK, 0)
    n_used = pad_ends[-1:] // EXPERT_BLOCK
    i32 = lambda a: a.astype(jnp.int32)
    return dest, i32(pad_ends), (e_of, i32(row0), i32(nrb), i32(n_used))


def _layer(x2d, p2d, B, S, norm_mix, w_in, lb, hgrn_out_norm, gmlp_ln_g, gmlp_ln_b, w_spatial,
           b_spatial, w_out, norm_ffn, w_router, b_router, w_gate_up, b_gate_up, w_down, b_down,
           norm_ple, w_ple_gate, w_ple_proj, g_out):
    T = x2d.shape[0]
    z = _inproj(x2d, norm_mix, w_in.astype(BF16))
    o = _hgrn(z, lb, hgrn_out_norm, B, S)
    sg = _gmlp(z, gmlp_ln_g, gmlp_ln_b, w_spatial, b_spatial)
    x1, h2, ri, rg, counts = _outproj(o, sg, x2d, w_out, norm_ffn, w_router, b_router)

    n_rows = T * TOP_K + N_EXPERTS * EXPERT_BLOCK
    n_items = N_EXPERTS + T * TOP_K // EXPERT_ROWS
    dest, pad_ends, items = _routing_tables(ri, counts, n_items)
    xs = _dispatch(dest, pad_ends, h2, n_rows)
    ys = _experts(items, xs, w_gate_up, w_down, b_gate_up, b_down, EXPERT_FC, EXPERT_NC)
    return _combine(dest, x1, rg, p2d, ys, norm_ple, w_ple_gate, w_ple_proj, g_out)


def kernel(x, p, norm_mix, w_in, lb_logits, hgrn_out_norm, gmlp_ln_g, gmlp_ln_b, w_spatial, b_spatial, w_out, norm_ffn, w_router, b_router, w_gate_up, b_gate_up, w_down, b_down, norm_ple, w_ple_gate, w_ple_proj, norm_final):
    B, S, D = x.shape
    depth = p.shape[0]
    assert depth == 1, "the final norm is fused into the last layer's combine kernel"
    lower_bounds = jnp.cumsum(jax.nn.softmax(lb_logits.astype(F32), axis=0), axis=0)
    out = _layer(x.reshape(B * S, D), p[0].reshape(B * S, -1), B, S, norm_mix[0], w_in[0],
                 lower_bounds[0], hgrn_out_norm[0], gmlp_ln_g[0], gmlp_ln_b[0], w_spatial[0],
                 b_spatial[0], w_out[0], norm_ffn[0], w_router[0], b_router[0], w_gate_up[0],
                 b_gate_up[0], w_down[0], b_down[0], norm_ple[0], w_ple_gate[0], w_ple_proj[0],
                 norm_final)
    return out.reshape(B, S, D)
```

```python
import functools
import math

import numpy as np
import jax
import jax.numpy as jnp
from jax import lax
from jax.experimental import pallas as pl
from jax.experimental.pallas import tpu as pltpu

F32 = jnp.float32
BF16 = jnp.bfloat16

HG_HEADS = 8
HG_D = 128
GM_GROUPS = 8
GM_CH = 128
GM_CHUNK = 128
N_EXPERTS = 32
TOP_K = 4
SWIGLU_LIMIT = 7.0
SWIGLU_ALPHA = 1.702
RMS_EPS = 1e-6
LN_EPS = 1e-5

LANES = 128
HGRN_TILE = 256
HGRN_HEADS_PER_STEP = 2
EXPERT_BLOCK = 256
EXPERT_ROWS = 2560
EXPERT_FC = 512
EXPERT_NC = 512
EXPERT_YSLOTS = 2
VMEM_LIMIT = 56 * 1024 * 1024


def _cparams(n_axes):
    return pltpu.CompilerParams(dimension_semantics=("arbitrary",) * n_axes,
                                vmem_limit_bytes=VMEM_LIMIT)


def _rms(x, g, eps):
    return x * lax.rsqrt(jnp.mean(x * x, axis=-1, keepdims=True) + eps) * g


def _dot(a, b):
    return jnp.dot(a, b, preferred_element_type=F32)


def _dot_nt(a, b):
    return lax.dot_general(a, b, (((1,), (1,)), ((), ())), preferred_element_type=F32)


def _bf16_bits(x):
    return pltpu.bitcast(x.astype(BF16).astype(F32), jnp.uint32) & jnp.uint32(0xFFFF0000)


def _pack_bf16_pair(lo, hi):
    return (_bf16_bits(lo) >> 16) | _bf16_bits(hi)


def _unpack_bf16_pair(w):
    lo = pltpu.bitcast(w << 16, F32).astype(BF16)
    hi = pltpu.bitcast(w & jnp.uint32(0xFFFF0000), F32).astype(BF16)
    return lo, hi


def _gelu(x):
    return 0.5 * x * (1.0 + lax.erf(x * (1.0 / math.sqrt(2.0))))


def _inproj_body(x_ref, g_ref, w_ref, z_ref, h_scr):
    @pl.when(pl.program_id(1) == 0)
    def _():
        h_scr[...] = _rms(x_ref[...], g_ref[...], RMS_EPS).astype(BF16)

    z_ref[...] = _dot(h_scr[...], w_ref[...])


def _inproj(x2d, g, w, tm=1024, tn=1024):
    T, D = x2d.shape
    N = w.shape[1]
    return pl.pallas_call(
        _inproj_body,
        grid=(T // tm, N // tn),
        in_specs=[pl.BlockSpec((tm, D), lambda i, j: (i, 0)),
                  pl.BlockSpec((1, D), lambda i, j: (0, 0)),
                  pl.BlockSpec((D, tn), lambda i, j: (0, j))],
        out_specs=pl.BlockSpec((tm, tn), lambda i, j: (i, j)),
        out_shape=jax.ShapeDtypeStruct((T, N), F32),
        scratch_shapes=[pltpu.VMEM((tm, D), BF16)],
        compiler_params=_cparams(2),
        name="inproj",
    )(x2d, g.reshape(1, D), w)


HGRN_MIN_BROADCAST = 8


def _hgrn_constants(C):
    nlev = int(math.log2(C))
    r = np.arange(C)
    mats = [(r[None, :] <= r[:, None])]
    level = np.full((C, C), -1, np.int32)
    level[r, r] = 0
    for l in range(1, nlev + 1):
        L = C >> l
        pos = r % (2 * L)
        mid = (r // (2 * L)) * (2 * L) + L - 1
        second = pos >= L
        if L < HGRN_MIN_BROADCAST:
            mats.append(np.where(second[:, None],
                                 (r[None, :] > mid[:, None]) & (r[None, :] <= r[:, None]),
                                 (r[None, :] > r[:, None]) & (r[None, :] <= mid[:, None])))
        same = (r[:, None] // (2 * L)) == (r[None, :] // (2 * L))
        level[same & second[:, None] & (~second)[None, :]] = l
    h = C // 2
    assert np.array_equal(level[:h, :h], level[h:, h:])
    w = np.concatenate(mats, axis=0).astype(np.float32)
    return jnp.asarray(w, BF16), jnp.asarray(level[:h, :h]), nlev


def _hgrn_head(q, f_logit, v, out_gate, lb, gn, st, w_ref, lv, nlev):
    C = q.shape[0]
    h = C // 2
    q = q * jax.nn.sigmoid(q)
    f = lb + (1.0 - lb) * jax.nn.sigmoid(f_logit)
    g = jnp.log(f)
    k = 1.0 - f
    vb = v.astype(BF16)

    g1 = g.astype(BF16)
    r1 = g - g1.astype(F32)
    g2 = r1.astype(BF16)
    g3 = (r1 - g2.astype(F32)).astype(BF16)
    gcat = jnp.concatenate([g1, g2, g3], axis=1)

    def range_sum(i):
        e3 = _dot(w_ref[i * C:(i + 1) * C, :], gcat)
        return e3[:, :HG_D] + e3[:, HG_D:2 * HG_D] + e3[:, 2 * HG_D:]

    b = range_sum(0)
    b_last = b[C - 1:C, :]

    def level_exponent(l):
        L = C >> l
        if L < HGRN_MIN_BROADCAST:
            return range_sum(1 + (l - (nlev + 1 - int(math.log2(HGRN_MIN_BROADCAST)))))
        mids = [jnp.broadcast_to(b[s + L - 1:s + L, :], (2 * L, HG_D)) for s in range(0, C, 2 * L)]
        mid = mids[0] if len(mids) == 1 else jnp.concatenate(mids, axis=0)
        return -jnp.abs(b - mid)

    def scaled(l):
        e = jnp.exp(level_exponent(l))
        return (q * e).astype(BF16), (k * e).astype(BF16)

    ql, kl = scaled(1)
    a_cross = _dot_nt(ql[h:], kl[:h])
    qb, kb = q.astype(BF16), k.astype(BF16)
    a_top = jnp.where(lv == 0, _dot_nt(qb[:h], kb[:h]), 0.0)
    a_bot = jnp.where(lv == 0, _dot_nt(qb[h:], kb[h:]), 0.0)
    for l in range(2, nlev + 1):
        ql, kl = scaled(l)
        a_top = jnp.where(lv == l, _dot_nt(ql[:h], kl[:h]), a_top)
        a_bot = jnp.where(lv == l, _dot_nt(ql[h:], kl[h:]), a_bot)

    o_top = _dot(a_top.astype(BF16), vb[:h])
    o_bot = _dot(a_cross.astype(BF16), vb[:h]) + _dot(a_bot.astype(BF16), vb[h:])
    o = _dot_nt((q * jnp.exp(b)).astype(BF16), st.astype(BF16))
    o = o + jnp.concatenate([o_top, o_bot], axis=0)

    khat = (k * jnp.exp(b_last - b)).astype(BF16)
    st_new = st * jnp.exp(b_last) + _dot(v.T.astype(BF16), khat)
    o = _rms(o, gn, RMS_EPS) * jax.nn.sigmoid(out_gate)
    return o.astype(BF16), st_new


def _hgrn_body(q_ref, f_ref, v_ref, og_ref, lb_ref, gn_ref, w_ref, lv_ref, o_ref, st_scr, *, nlev):
    @pl.when(pl.program_id(2) == 0)
    def _():
        st_scr[...] = jnp.zeros_like(st_scr)

    lv = lv_ref[...]
    for hh in range(st_scr.shape[0]):
        lanes = slice(hh * HG_D, (hh + 1) * HG_D)
        o, st_new = _hgrn_head(q_ref[:, lanes], f_ref[:, lanes], v_ref[:, lanes], og_ref[:, lanes],
                               lb_ref[hh], gn_ref[hh], st_scr[hh], w_ref, lv, nlev)
        o_ref[:, lanes] = o
        st_scr[hh] = st_new


def _hgrn(z, lb, gn, B, S):
    C = HGRN_TILE
    H = HG_HEADS
    n_s = S // C
    w, lv, nlev = _hgrn_constants(C)
    hp = HGRN_HEADS_PER_STEP
    row = lambda b, h, s: b * n_s + s
    col = lambda off: pl.BlockSpec((C, hp * HG_D),
                                   lambda b, h, s: (row(b, h, s), off * (H // hp) + h))
    per_head = pl.BlockSpec((hp, 1, HG_D), lambda b, h, s: (h, 0, 0))
    return pl.pallas_call(
        functools.partial(_hgrn_body, nlev=nlev),
        grid=(B, H // hp, n_s),
        in_specs=[col(0), col(1), col(2), col(3), per_head, per_head,
                  pl.BlockSpec(w.shape, lambda b, h, s: (0, 0)),
                  pl.BlockSpec(lv.shape, lambda b, h, s: (0, 0))],
        out_specs=pl.BlockSpec((C, hp * HG_D), lambda b, h, s: (row(b, h, s), h)),
        out_shape=jax.ShapeDtypeStruct((B * S, H * HG_D), BF16),
        scratch_shapes=[pltpu.VMEM((hp, HG_D, HG_D), F32)],
        compiler_params=_cparams(3),
        name="hgrn",
    )(z, z, z, z, lb.reshape(H, 1, HG_D), gn.reshape(H, 1, HG_D), w, lv)


def _gmlp_body(u_ref, v_ref, lng_ref, lnb_ref, ws_ref, bias_ref, o_ref, w_scr):
    @pl.when(pl.program_id(0) == 0)
    def _():
        r = lax.broadcasted_iota(jnp.int32, ws_ref.shape, 1)
        c = lax.broadcasted_iota(jnp.int32, ws_ref.shape, 2)
        w_scr[...] = jnp.where(r >= c, ws_ref[...], 0.0).astype(BF16)

    v = _gelu(v_ref[...])
    mu = jnp.mean(v, axis=-1, keepdims=True)
    vc = v - mu
    vn = vc * lax.rsqrt(jnp.mean(vc * vc, axis=-1, keepdims=True) + LN_EPS)
    vn = (vn * lng_ref[...] + lnb_ref[...]).astype(BF16)
    n_chunks = u_ref.shape[0] // GM_CHUNK
    for c in range(n_chunks):
        rows = slice(c * GM_CHUNK, (c + 1) * GM_CHUNK)
        for g in range(GM_GROUPS):
            cols = slice(g * GM_CH, (g + 1) * GM_CH)
            mixed = _dot(w_scr[g], vn[rows, cols]) + bias_ref[:, cols]
            o_ref[rows, cols] = (_gelu(u_ref[rows, cols]) * mixed).astype(BF16)


def _gmlp(z, ln_g, ln_b, w_s, b_s, tg=512):
    T = z.shape[0]
    W = GM_GROUPS * GM_CH
    u_col = 4 * HG_HEADS * HG_D // W
    bias = jnp.repeat(b_s.T, GM_CH, axis=1)
    return pl.pallas_call(
        _gmlp_body,
        grid=(T // tg,),
        in_specs=[pl.BlockSpec((tg, W), lambda i: (i, u_col)),
                  pl.BlockSpec((tg, W), lambda i: (i, u_col + 1)),
                  pl.BlockSpec((1, W), lambda i: (0, 0)),
                  pl.BlockSpec((1, W), lambda i: (0, 0)),
                  pl.BlockSpec(w_s.shape, lambda i: (0, 0, 0)),
                  pl.BlockSpec(bias.shape, lambda i: (0, 0))],
        out_specs=pl.BlockSpec((tg, W), lambda i: (i, 0)),
        out_shape=jax.ShapeDtypeStruct((T, W), BF16),
        scratch_shapes=[pltpu.VMEM(w_s.shape, BF16)],
        compiler_params=_cparams(1),
        name="gmlp",
    )(z, z, ln_g.reshape(1, W), ln_b.reshape(1, W), w_s, bias)


def _outproj_body(o_ref, sg_ref, x_ref, wa_ref, wb_ref, g_ref, wr_ref, br_ref,
                  x1_ref, h_ref, ri_ref, rg_ref, cnt_ref, tri_scr, carry_scr):
    tm = x_ref.shape[0]

    @pl.when(pl.program_id(0) == 0)
    def _():
        r = lax.broadcasted_iota(jnp.int32, (tm, tm), 0)
        c = lax.broadcasted_iota(jnp.int32, (tm, tm), 1)
        tri_scr[...] = (c < r).astype(BF16)
        carry_scr[...] = jnp.zeros_like(carry_scr)

    x1 = x_ref[...] + _dot(o_ref[...], wa_ref[...]) + _dot(sg_ref[...], wb_ref[...])
    x1_ref[...] = x1
    h = _rms(x1, g_ref[...], RMS_EPS)
    half = h.shape[1] // 2
    h_ref[...] = _pack_bf16_pair(h[:, :half], h[:, half:])

    logits = jnp.dot(h, wr_ref[...], preferred_element_type=F32,
                     precision=lax.Precision.HIGHEST) + br_ref[...]
    lane = lax.broadcasted_iota(jnp.int32, (tm, LANES), 1)
    neg = jnp.float32(-jnp.inf)
    work = jnp.where(lane < N_EXPERTS, logits, neg)
    vals, idxs = [], []
    for _ in range(TOP_K):
        m = jnp.max(work, axis=-1, keepdims=True)
        i = jnp.min(jnp.where(work == m, lane, LANES), axis=-1, keepdims=True)
        vals.append(m)
        idxs.append(i)
        work = jnp.where(lane == i, neg, work)
    es = [jnp.exp(m - vals[0]) for m in vals]
    denom = es[0] + es[1] + es[2] + es[3]

    onehot = jnp.zeros((tm, LANES), F32)
    for i in idxs:
        onehot = onehot + (lane == i).astype(F32)
    before = _dot(tri_scr[...], onehot.astype(BF16)) + carry_scr[0:1, :]
    carry = carry_scr[0:1, :] + jnp.sum(onehot, axis=0, keepdims=True)
    carry_scr[...] = jnp.broadcast_to(carry, carry_scr.shape)
    cnt_ref[...] = jnp.broadcast_to(carry, cnt_ref.shape).astype(jnp.int32)

    ri = jnp.zeros((tm, LANES), jnp.int32)
    rg = jnp.zeros((tm, LANES), F32)
    for kk in range(TOP_K):
        rank = jnp.sum(jnp.where(lane == idxs[kk], before, 0.0), axis=-1, keepdims=True)
        ri = jnp.where(lane == kk, idxs[kk], ri)
        ri = jnp.where(lane == TOP_K + kk, rank.astype(jnp.int32), ri)
        rg = jnp.where(lane == kk, es[kk] / denom, rg)
    ri_ref[...] = ri
    rg_ref[...] = rg


def _outproj(o, sg, x2d, w_out, g, w_router, b_router, tm=512):
    T, D = x2d.shape
    Wh = o.shape[1]
    wa = w_out[:Wh].astype(BF16)
    wb = w_out[Wh:].astype(BF16)
    wr = jnp.zeros((D, LANES), F32).at[:, :N_EXPERTS].set(w_router)
    br = jnp.zeros((1, LANES), F32).at[0, :N_EXPERTS].set(b_router)
    row = lambda i: (i, 0)
    fixed = lambda i: (0, 0)
    return pl.pallas_call(
        _outproj_body,
        grid=(T // tm,),
        in_specs=[pl.BlockSpec((tm, Wh), row), pl.BlockSpec((tm, sg.shape[1]), row),
                  pl.BlockSpec((tm, D), row),
                  pl.BlockSpec(wa.shape, fixed), pl.BlockSpec(wb.shape, fixed),
                  pl.BlockSpec((1, D), fixed), pl.BlockSpec(wr.shape, fixed),
                  pl.BlockSpec(br.shape, fixed)],
        out_specs=[pl.BlockSpec((tm, D), row), pl.BlockSpec((tm, D // 2), row),
                   pl.BlockSpec((tm, LANES), row), pl.BlockSpec((tm, LANES), row),
                   pl.BlockSpec((8, LANES), fixed)],
        out_shape=[jax.ShapeDtypeStruct((T, D), F32), jax.ShapeDtypeStruct((T, D // 2), jnp.uint32),
                   jax.ShapeDtypeStruct((T, LANES), jnp.int32),
                   jax.ShapeDtypeStruct((T, LANES), F32),
                   jax.ShapeDtypeStruct((8, LANES), jnp.int32)],
        scratch_shapes=[pltpu.VMEM((tm, tm), BF16), pltpu.VMEM((8, LANES), F32)],
        compiler_params=_cparams(1),
        name="outproj_router",
    )(o, sg, x2d, wa, wb, g.reshape(1, D), wr, br)


def _dispatch_body(dest_ref, ends_ref, h_ref, xs_ref, zero_scr, sem, zsem):
    tm = h_ref.shape[0]
    base = pl.program_id(0) * (tm * TOP_K)

    @pl.when(pl.program_id(0) == 0)
    def _():
        zero_scr[...] = jnp.zeros_like(zero_scr)

        def last_block(e, op):
            end = ends_ref[e]
            prev = ends_ref[e - 1] if e else 0

            @pl.when(end > prev)
            def _():
                start = pl.multiple_of(end - EXPERT_BLOCK, EXPERT_BLOCK)
                op(pltpu.make_async_copy(zero_scr, xs_ref.at[pl.ds(start, EXPERT_BLOCK)], zsem))

        for e in range(N_EXPERTS):
            last_block(e, lambda cp: cp.start())
        for e in range(N_EXPERTS):
            last_block(e, lambda cp: cp.wait())

        def tail_copy(b):
            start = pl.multiple_of(b * EXPERT_BLOCK, EXPERT_BLOCK)
            return pltpu.make_async_copy(zero_scr, xs_ref.at[pl.ds(start, EXPERT_BLOCK)], zsem)

        first = ends_ref[N_EXPERTS - 1] // EXPERT_BLOCK
        n_blocks = xs_ref.shape[0] // EXPERT_BLOCK
        lax.fori_loop(first, n_blocks, lambda b, c: (tail_copy(b).start(), c)[1], 0)
        lax.fori_loop(first, n_blocks, lambda b, c: (tail_copy(b).wait(), c)[1], 0)

    def row_copy(r, d):
        return pltpu.make_async_copy(h_ref.at[pl.ds(r, 1)], xs_ref.at[pl.ds(d, 1)], sem)

    def issue(r, carry):
        for kk in range(TOP_K):
            row_copy(r, dest_ref[base + r * TOP_K + kk]).start(priority=kk % 2)
        return carry

    lax.fori_loop(0, tm, issue, 0)

    def drain(r, carry):
        for kk in range(TOP_K):
            row_copy(0, 0).wait()
        return carry

    lax.fori_loop(0, tm, drain, 0)


def _dispatch(dest_flat, pad_ends, h, n_rows, tm=256):
    T, W = h.shape
    return pl.pallas_call(
        _dispatch_body,
        grid_spec=pltpu.PrefetchScalarGridSpec(
            num_scalar_prefetch=2,
            grid=(T // tm,),
            in_specs=[pl.BlockSpec((tm, W), lambda i, dest, ends: (i, 0))],
            out_specs=pl.BlockSpec(memory_space=pl.ANY),
            scratch_shapes=[pltpu.VMEM((EXPERT_BLOCK, W), h.dtype),
                            pltpu.SemaphoreType.DMA(()), pltpu.SemaphoreType.DMA(())]),
        out_shape=jax.ShapeDtypeStruct((n_rows, W), h.dtype),
        compiler_params=_cparams(1),
        name="dispatch",
    )(dest_flat, pad_ends, h)


def _swiglu_deinterleaved(gu, fc):
    even = lax.broadcasted_iota(jnp.int32, (gu.shape[0], LANES), 1) % 2 == 0
    acts = []
    for c in range(fc // LANES):
        a = gu[:, c * LANES:(c + 1) * LANES]
        b = gu[:, fc + c * LANES:fc + (c + 1) * LANES]
        gate = jnp.where(even, a, pltpu.roll(b, 1, axis=1))
        up = jnp.where(even, pltpu.roll(a, LANES - 1, axis=1), b)
        gate = jnp.minimum(gate, SWIGLU_LIMIT)
        up = jnp.clip(up, -SWIGLU_LIMIT, SWIGLU_LIMIT)
        acts.append(((up + 1.0) * gate * jax.nn.sigmoid(SWIGLU_ALPHA * gate)).astype(BF16))
    return jnp.concatenate(acts, axis=1)


def _interleave_rows_bf16(a, b):
    return pltpu.bitcast(_pack_bf16_pair(a, b), BF16)


def _expert_body(ie_ref, ir_ref, in_ref, nu_ref,
                 xs_ref, wgu_ref, wdn_ref, bgu_ref, bdn_ref, ys_ref,
                 xres, act, wgu_bf, wdn_bf, ystage, sem_x, sem_y, *, n_gu, n_dn):
    del ie_ref
    i = pl.program_id(0)
    j = pl.program_id(1)
    nrb = in_ref[i]
    row0 = ir_ref[i]
    fc = act.shape[2]
    nc = ystage.shape[2]
    blk = EXPERT_BLOCK

    def rows(rb, first=0):
        return pl.ds(pl.multiple_of(first + rb * blk, blk), blk)

    @pl.when((i == 0) & (j == 0))
    def _():
        ystage[0] = jnp.zeros(ystage.shape[1:], ystage.dtype)
        n_blocks = ys_ref.shape[0] // blk

        def tail_copy(b, cc):
            return pltpu.make_async_copy(
                ystage.at[0, pl.ds(0, blk)], ys_ref.at[rows(b), pl.ds(cc * nc, nc)], sem_y.at[0])

        def tail_start(b, c):
            for cc in range(n_dn):
                tail_copy(b, cc).start()
            return c

        def tail_wait(b, c):
            for cc in range(n_dn):
                tail_copy(b, cc).wait()
            return c

        lax.fori_loop(nu_ref[0], n_blocks, tail_start, 0)
        lax.fori_loop(nu_ref[0], n_blocks, tail_wait, 0)

    def x_copy(rb):
        return pltpu.make_async_copy(xs_ref.at[rows(rb, row0)], xres.at[rows(rb)], sem_x)

    def for_blocks(fn):
        lax.fori_loop(0, nrb, lambda rb, c: (fn(rb), c)[1], 0)

    n_pairs = nrb // 2
    odd = nrb % 2

    def unit_rows(u, size, first=0):
        return pl.ds(pl.multiple_of(first + u * (2 * blk), blk), size)

    def for_units(fn):
        lax.fori_loop(0, n_pairs, lambda u, c: (fn(u, 2 * blk), c)[1], 0)

        @pl.when(odd == 1)
        def _():
            fn(n_pairs, blk)

    @pl.when((nrb > 0) & (j < n_gu))
    def _():
        @pl.when(j == 0)
        def _():
            for_blocks(lambda rb: x_copy(rb).start())

        wgu_bf[...] = wgu_ref[...].astype(BF16)

        @pl.when(j == 0)
        def _():
            for_blocks(lambda rb: x_copy(rb).wait())

        def unit(u, size):
            x = jnp.concatenate(_unpack_bf16_pair(xres[unit_rows(u, size), :]), axis=1)
            gu = _dot(x, wgu_bf[...]) + bgu_ref[...]
            act[j, unit_rows(u, size), :] = _swiglu_deinterleaved(gu, fc)

        for_units(unit)

    @pl.when((nrb > 0) & (j >= n_gu))
    def _():
        jd = j - n_gu
        n_slots = ystage.shape[0]
        for jf in range(n_gu):
            for c in range(fc // LANES):
                a0 = jf * fc + c * (LANES // 2)
                b0 = a0 + fc // 2
                wdn_bf[jf * fc + c * LANES:jf * fc + (c + 1) * LANES, :] = _interleave_rows_bf16(
                    wdn_ref[a0:a0 + LANES // 2, :], wdn_ref[b0:b0 + LANES // 2, :])

        def y_copy(u, size):
            slot = u % n_slots
            return pltpu.make_async_copy(
                ystage.at[slot, pl.ds(0, size)],
                ys_ref.at[unit_rows(u, size, row0), pl.ds(pl.multiple_of(jd * nc, nc), nc)],
                sem_y.at[slot])

        def unit(u, size):
            @pl.when(u >= n_slots)
            def _():
                y_copy(u - n_slots, 2 * blk).wait()

            a = jnp.concatenate([act[jf, unit_rows(u, size), :] for jf in range(n_gu)], axis=1)
            y = _dot(a, wdn_bf[...]) + bdn_ref[...]
            ystage[u % n_slots, 0:size, :] = _pack_bf16_pair(y[:, :nc], y[:, nc:])
            y_copy(u, size).start()

        for_units(unit)
        n_units = n_pairs + odd
        for back in range(1, n_slots + 1):
            @pl.when((n_units >= back) & ((odd == 0) | (back > 1)))
            def _():
                y_copy(n_units - back, 2 * blk).wait()

        @pl.when(odd == 1)
        def _():
            y_copy(n_units - 1, blk).wait()


def _experts(items, xs, w_gate_up, w_down, b_gate_up, b_down, fc, nc):
    item_e, item_row0, item_nrb, n_used = items
    n_rows = xs.shape[0]
    E, D, F2 = w_gate_up.shape
    F = F2 // 2
    n_gu = F // fc
    n_dn = D // nc
    n_items = item_e.shape[0]

    def gu_chunk(i, j, ie, ir, inr, nu):
        return jnp.where(inr[i] > 0, jnp.minimum(j, n_gu - 1), n_gu - 1)

    def dn_chunk(i, j, ie, ir, inr, nu):
        return jnp.where(inr[i] > 0, jnp.maximum(j - n_gu, 0), n_dn - 1)

    return pl.pallas_call(
        functools.partial(_expert_body, n_gu=n_gu, n_dn=n_dn),
        grid_spec=pltpu.PrefetchScalarGridSpec(
            num_scalar_prefetch=4,
            grid=(n_items, n_gu + n_dn),
            in_specs=[
                pl.BlockSpec(memory_space=pl.ANY),
                pl.BlockSpec((None, D, 2 * fc), lambda i, j, ie, *a: (ie[i], 0, gu_chunk(i, j, ie, *a))),
                pl.BlockSpec((None, F, nc), lambda i, j, ie, *a: (ie[i], 0, dn_chunk(i, j, ie, *a))),
                pl.BlockSpec((None, 1, 2 * fc), lambda i, j, ie, *a: (ie[i], 0, gu_chunk(i, j, ie, *a))),
                pl.BlockSpec((None, 1, nc), lambda i, j, ie, *a: (ie[i], 0, dn_chunk(i, j, ie, *a))),
            ],
            out_specs=pl.BlockSpec(memory_space=pl.ANY),
            scratch_shapes=[
                pltpu.VMEM((EXPERT_ROWS, D // 2), jnp.uint32),
                pltpu.VMEM((n_gu, EXPERT_ROWS, fc), BF16),
                pltpu.VMEM((D, 2 * fc), BF16), pltpu.VMEM((F, nc), BF16),
                pltpu.VMEM((EXPERT_YSLOTS, 2 * EXPERT_BLOCK, nc // 2), jnp.uint32),
                pltpu.SemaphoreType.DMA(()), pltpu.SemaphoreType.DMA((EXPERT_YSLOTS,))]),
        out_shape=jax.ShapeDtypeStruct((n_rows, D // 2), jnp.uint32),
        compiler_params=_cparams(2),
        name="experts",
    )(item_e, item_row0, item_nrb, n_used, xs, w_gate_up, w_down, b_gate_up[:, None, :],
      b_down[:, None, :])


def _combine_body(dest_ref, x1_ref, rg_ref, p_ref, ys_ref, gp_ref, wpg_ref, wpp_ref, gf_ref,
                  out_ref, buf, sem):
    tm = x1_ref.shape[0]
    i = pl.program_id(0)
    slot = i % 2

    def row_copy(s, r, kk, d):
        return pltpu.make_async_copy(ys_ref.at[pl.ds(d, 1)], buf.at[s, kk, pl.ds(r, 1)], sem.at[s])

    def gather(step, s):
        base = step * (tm * TOP_K)

        def issue(r, carry):
            for kk in range(TOP_K):
                row_copy(s, r, kk, dest_ref[base + r * TOP_K + kk]).start(priority=kk % 2)
            return carry

        lax.fori_loop(0, tm, issue, 0)

    @pl.when(i == 0)
    def _():
        gather(0, 0)

    @pl.when(i + 1 < pl.num_programs(0))
    def _():
        gather(i + 1, 1 - slot)

    def drain(r, carry):
        for kk in range(TOP_K):
            row_copy(slot, 0, kk, 0).wait()
        return carry

    lax.fori_loop(0, tm, drain, 0)

    def unpacked(w):
        pc = EXPERT_NC // 2
        parts = []
        for c in range(w.shape[1] // pc):
            chunk = w[:, c * pc:(c + 1) * pc]
            parts.append(pltpu.bitcast(chunk << 16, F32))
            parts.append(pltpu.bitcast(chunk & jnp.uint32(0xFFFF0000), F32))
        return jnp.concatenate(parts, axis=1)

    gates = rg_ref[...]
    x2 = x1_ref[...]
    for kk in range(TOP_K):
        x2 = x2 + gates[:, kk:kk + 1] * unpacked(buf[slot, kk])
    hp = _rms(x2, gp_ref[...], RMS_EPS).astype(BF16)
    gate = jax.nn.sigmoid(_dot(hp, wpg_ref[...]))
    proj = _dot(p_ref[...].astype(BF16), wpp_ref[...])
    x3 = x2 + gate * proj
    out_ref[...] = _rms(x3, gf_ref[...], RMS_EPS)


def _combine(dest_flat, x1, rg, p2d, ys, g_ple, w_gate, w_proj, g_final, tm=256):
    T, D = x1.shape
    P = p2d.shape[1]
    row = lambda i, dest: (i, 0)
    fixed = lambda i, dest: (0, 0)
    return pl.pallas_call(
        _combine_body,
        grid_spec=pltpu.PrefetchScalarGridSpec(
            num_scalar_prefetch=1,
            grid=(T // tm,),
            in_specs=[pl.BlockSpec((tm, D), row), pl.BlockSpec((tm, LANES), row),
                      pl.BlockSpec((tm, P), row), pl.BlockSpec(memory_space=pl.ANY),
                      pl.BlockSpec((1, D), fixed), pl.BlockSpec((D, D), fixed),
                      pl.BlockSpec((P, D), fixed), pl.BlockSpec((1, D), fixed)],
            out_specs=pl.BlockSpec((tm, D), row),
            scratch_shapes=[pltpu.VMEM((2, TOP_K, tm, D // 2), jnp.uint32),
                            pltpu.SemaphoreType.DMA((2,))]),
        out_shape=jax.ShapeDtypeStruct((T, D), F32),
        compiler_params=_cparams(1),
        name="combine_ple",
    )(dest_flat, x1, rg, p2d, ys, g_ple.reshape(1, D), w_gate.astype(BF16),
      w_proj.astype(BF16), g_final.reshape(1, D))


def _routing_tables(ri, counts_block, n_items):
    counts = counts_block[0, :N_EXPERTS]
    padded = (counts + EXPERT_BLOCK - 1) // EXPERT_BLOCK * EXPERT_BLOCK
    pad_ends = jnp.cumsum(padded)
    pad_starts = pad_ends - padded
    idx = ri[:, :TOP_K]
    rank = ri[:, TOP_K:2 * TOP_K]
    dest = (pad_starts[idx] + rank).reshape(-1).astype(jnp.int32)

    per_expert = (padded + EXPERT_ROWS - 1) // EXPERT_ROWS
    item_ends = jnp.cumsum(per_expert)
    total = item_ends[-1]
    ids = jnp.arange(n_items, dtype=jnp.int32)
    e_of = jnp.sum(jnp.minimum(ids, total - 1)[:, None] >= item_ends[None, :], axis=1)
    e_of = jnp.clip(e_of, 0, N_EXPERTS - 1).astype(jnp.int32)
    local = ids - (item_ends - per_expert)[e_of]
    active = ids < total
    row0 = jnp.where(active, pad_starts[e_of] + local * EXPERT_ROWS, 0)
    left = jnp.clip(padded[e_of] - local * EXPERT_ROWS, 0, EXPERT_ROWS)
    nrb = jnp.where(active, left // EXPERT_BLOCK, 0)
    n_used = pad_ends[-1:] // EXPERT_BLOCK
    i32 = lambda a: a.astype(jnp.int32)
    return dest, i32(pad_ends), (e_of, i32(row0), i32(nrb), i32(n_used))


def _layer(x2d, p2d, B, S, norm_mix, w_in, lb, hgrn_out_norm, gmlp_ln_g, gmlp_ln_b, w_spatial,
           b_spatial, w_out, norm_ffn, w_router, b_router, w_gate_up, b_gate_up, w_down, b_down,
           norm_ple, w_ple_gate, w_ple_proj, g_out):
    T = x2d.shape[0]
    z = _inproj(x2d, norm_mix, w_in.astype(BF16))
    o = _hgrn(z, lb, hgrn_out_norm, B, S)
    sg = _gmlp(z, gmlp_ln_g, gmlp_ln_b, w_spatial, b_spatial)
    x1, h2, ri, rg, counts = _outproj(o, sg, x2d, w_out, norm_ffn, w_router, b_router)

    n_rows = T * TOP_K + N_EXPERTS * EXPERT_BLOCK
    n_items = N_EXPERTS + T * TOP_K // EXPERT_ROWS
    dest, pad_ends, items = _routing_tables(ri, counts, n_items)
    xs = _dispatch(dest, pad_ends, h2, n_rows)
    ys = _experts(items, xs, w_gate_up, w_down, b_gate_up, b_down, EXPERT_FC, EXPERT_NC)
    return _combine(dest, x1, rg, p2d, ys, norm_ple, w_ple_gate, w_ple_proj, g_out)


def kernel(x, p, norm_mix, w_in, lb_logits, hgrn_out_norm, gmlp_ln_g, gmlp_ln_b, w_spatial, b_spatial, w_out, norm_ffn, w_router, b_router, w_gate_up, b_gate_up, w_down, b_down, norm_ple, w_ple_gate, w_ple_proj, norm_final):
    B, S, D = x.shape
    depth = p.shape[0]
    assert depth == 1, "the final norm is fused into the last layer's combine kernel"
    lower_bounds = jnp.cumsum(jax.nn.softmax(lb_logits.astype(F32), axis=0), axis=0)
    out = _layer(x.reshape(B * S, D), p[0].reshape(B * S, -1), B, S, norm_mix[0], w_in[0],
                 lower_bounds[0], hgrn_out_norm[0], gmlp_ln_g[0], gmlp_ln_b[0], w_spatial[0],
                 b_spatial[0], w_out[0], norm_ffn[0], w_router[0], b_router[0], w_gate_up[0],
                 b_gate_up[0], w_down[0], b_down[0], norm_ple[0], w_ple_gate[0], w_ple_proj[0],
                 norm_final)
    return out.reshape(B, S, D)
```

```python
import functools
import math

import numpy as np
import jax
import jax.numpy as jnp
from jax import lax
from jax.experimental import pallas as pl
from jax.experimental.pallas import tpu as pltpu

F32 = jnp.float32
BF16 = jnp.bfloat16

HG_HEADS = 8
HG_D = 128
GM_GROUPS = 8
GM_CH = 128
GM_CHUNK = 128
N_EXPERTS = 32
TOP_K = 4
SWIGLU_LIMIT = 7.0
SWIGLU_ALPHA = 1.702
RMS_EPS = 1e-6
LN_EPS = 1e-5

LANES = 128
HGRN_TILE = 256
HGRN_HEADS_PER_STEP = 2
EXPERT_BLOCK = 256
EXPERT_ROWS = 2560
EXPERT_FC = 512
EXPERT_NC = 512
EXPERT_YSLOTS = 2
VMEM_LIMIT = 56 * 1024 * 1024
ROW_DMA_UNROLL = 8


def _cparams(n_axes):
    return pltpu.CompilerParams(dimension_semantics=("arbitrary",) * n_axes,
                                vmem_limit_bytes=VMEM_LIMIT)


def _rms(x, g, eps):
    return x * lax.rsqrt(jnp.mean(x * x, axis=-1, keepdims=True) + eps) * g


def _dot(a, b):
    return jnp.dot(a, b, preferred_element_type=F32)


def _dot_nt(a, b):
    return lax.dot_general(a, b, (((1,), (1,)), ((), ())), preferred_element_type=F32)


def _bf16_bits(x):
    return pltpu.bitcast(x.astype(BF16).astype(F32), jnp.uint32) & jnp.uint32(0xFFFF0000)


def _pack_bf16_pair(lo, hi):
    return (_bf16_bits(lo) >> 16) | _bf16_bits(hi)


def _unpack_bf16_pair(w):
    lo = pltpu.bitcast(w << 16, F32).astype(BF16)
    hi = pltpu.bitcast(w & jnp.uint32(0xFFFF0000), F32).astype(BF16)
    return lo, hi


def _gelu(x):
    return 0.5 * x * (1.0 + lax.erf(x * (1.0 / math.sqrt(2.0))))


def _inproj_body(x_ref, g_ref, w_ref, z_ref, h_scr):
    @pl.when(pl.program_id(1) == 0)
    def _():
        h_scr[...] = _rms(x_ref[...], g_ref[...], RMS_EPS).astype(BF16)

    z_ref[...] = _dot(h_scr[...], w_ref[...])


def _inproj(x2d, g, w, tm=1024, tn=1024):
    T, D = x2d.shape
    N = w.shape[1]
    return pl.pallas_call(
        _inproj_body,
        grid=(T // tm, N // tn),
        in_specs=[pl.BlockSpec((tm, D), lambda i, j: (i, 0)),
                  pl.BlockSpec((1, D), lambda i, j: (0, 0)),
                  pl.BlockSpec((D, tn), lambda i, j: (0, j))],
        out_specs=pl.BlockSpec((tm, tn), lambda i, j: (i, j)),
        out_shape=jax.ShapeDtypeStruct((T, N), F32),
        scratch_shapes=[pltpu.VMEM((tm, D), BF16)],
        compiler_params=_cparams(2),
        name="inproj",
    )(x2d, g.reshape(1, D), w)


HGRN_MIN_BROADCAST = 8


def _hgrn_constants(C):
    nlev = int(math.log2(C))
    r = np.arange(C)
    mats = [(r[None, :] <= r[:, None])]
    level = np.full((C, C), -1, np.int32)
    level[r, r] = 0
    for l in range(1, nlev + 1):
        L = C >> l
        pos = r % (2 * L)
        mid = (r // (2 * L)) * (2 * L) + L - 1
        second = pos >= L
        if L < HGRN_MIN_BROADCAST:
            mats.append(np.where(second[:, None],
                                 (r[None, :] > mid[:, None]) & (r[None, :] <= r[:, None]),
                                 (r[None, :] > r[:, None]) & (r[None, :] <= mid[:, None])))
        same = (r[:, None] // (2 * L)) == (r[None, :] // (2 * L))
        level[same & second[:, None] & (~second)[None, :]] = l
    h = C // 2
    assert np.array_equal(level[:h, :h], level[h:, h:])
    w = np.concatenate(mats, axis=0).astype(np.float32)
    return jnp.asarray(w, BF16), jnp.asarray(level[:h, :h]), nlev


def _hgrn_head(q, f_logit, v, out_gate, lb, gn, st, w_ref, lv, nlev):
    C = q.shape[0]
    h = C // 2
    q = q * jax.nn.sigmoid(q)
    f = lb + (1.0 - lb) * jax.nn.sigmoid(f_logit)
    g = jnp.log(f)
    k = 1.0 - f
    vb = v.astype(BF16)

    g1 = g.astype(BF16)
    r1 = g - g1.astype(F32)
    g2 = r1.astype(BF16)
    g3 = (r1 - g2.astype(F32)).astype(BF16)
    gcat = jnp.concatenate([g1, g2, g3], axis=1)

    def range_sum(i):
        e3 = _dot(w_ref[i * C:(i + 1) * C, :], gcat)
        return e3[:, :HG_D] + e3[:, HG_D:2 * HG_D] + e3[:, 2 * HG_D:]

    b = range_sum(0)
    b_last = b[C - 1:C, :]

    def level_exponent(l):
        L = C >> l
        if L < HGRN_MIN_BROADCAST:
            return range_sum(1 + (l - (nlev + 1 - int(math.log2(HGRN_MIN_BROADCAST)))))
        mids = [jnp.broadcast_to(b[s + L - 1:s + L, :], (2 * L, HG_D)) for s in range(0, C, 2 * L)]
        mid = mids[0] if len(mids) == 1 else jnp.concatenate(mids, axis=0)
        return -jnp.abs(b - mid)

    def scaled(l):
        e = jnp.exp(level_exponent(l))
        return (q * e).astype(BF16), (k * e).astype(BF16)

    ql, kl = scaled(1)
    a_cross = _dot_nt(ql[h:], kl[:h])
    qb, kb = q.astype(BF16), k.astype(BF16)
    a_top = jnp.where(lv == 0, _dot_nt(qb[:h], kb[:h]), 0.0)
    a_bot = jnp.where(lv == 0, _dot_nt(qb[h:], kb[h:]), 0.0)
    for l in range(2, nlev + 1):
        ql, kl = scaled(l)
        a_top = jnp.where(lv == l, _dot_nt(ql[:h], kl[:h]), a_top)
        a_bot = jnp.where(lv == l, _dot_nt(ql[h:], kl[h:]), a_bot)

    o_top = _dot(a_top.astype(BF16), vb[:h])
    o_bot = _dot(a_cross.astype(BF16), vb[:h]) + _dot(a_bot.astype(BF16), vb[h:])
    o = _dot_nt((q * jnp.exp(b)).astype(BF16), st.astype(BF16))
    o = o + jnp.concatenate([o_top, o_bot], axis=0)

    khat = (k * jnp.exp(b_last - b)).astype(BF16)
    st_new = st * jnp.exp(b_last) + _dot(v.T.astype(BF16), khat)
    o = _rms(o, gn, RMS_EPS) * jax.nn.sigmoid(out_gate)
    return o.astype(BF16), st_new


def _hgrn_body(q_ref, f_ref, v_ref, og_ref, lb_ref, gn_ref, w_ref, lv_ref, o_ref, st_scr, *, nlev):
    @pl.when(pl.program_id(2) == 0)
    def _():
        st_scr[...] = jnp.zeros_like(st_scr)

    lv = lv_ref[...]
    for hh in range(st_scr.shape[0]):
        lanes = slice(hh * HG_D, (hh + 1) * HG_D)
        o, st_new = _hgrn_head(q_ref[:, lanes], f_ref[:, lanes], v_ref[:, lanes], og_ref[:, lanes],
                               lb_ref[hh], gn_ref[hh], st_scr[hh], w_ref, lv, nlev)
        o_ref[:, lanes] = o
        st_scr[hh] = st_new


def _hgrn(z, lb, gn, B, S):
    C = HGRN_TILE
    H = HG_HEADS
    n_s = S // C
    w, lv, nlev = _hgrn_constants(C)
    hp = HGRN_HEADS_PER_STEP
    row = lambda b, h, s: b * n_s + s
    col = lambda off: pl.BlockSpec((C, hp * HG_D),
                                   lambda b, h, s: (row(b, h, s), off * (H // hp) + h))
    per_head = pl.BlockSpec((hp, 1, HG_D), lambda b, h, s: (h, 0, 0))
    return pl.pallas_call(
        functools.partial(_hgrn_body, nlev=nlev),
        grid=(B, H // hp, n_s),
        in_specs=[col(0), col(1), col(2), col(3), per_head, per_head,
                  pl.BlockSpec(w.shape, lambda b, h, s: (0, 0)),
                  pl.BlockSpec(lv.shape, lambda b, h, s: (0, 0))],
        out_specs=pl.BlockSpec((C, hp * HG_D), lambda b, h, s: (row(b, h, s), h)),
        out_shape=jax.ShapeDtypeStruct((B * S, H * HG_D), BF16),
        scratch_shapes=[pltpu.VMEM((hp, HG_D, HG_D), F32)],
        compiler_params=_cparams(3),
        name="hgrn",
    )(z, z, z, z, lb.reshape(H, 1, HG_D), gn.reshape(H, 1, HG_D), w, lv)


def _gmlp_body(u_ref, v_ref, lng_ref, lnb_ref, ws_ref, bias_ref, o_ref, w_scr):
    @pl.when(pl.program_id(0) == 0)
    def _():
        r = lax.broadcasted_iota(jnp.int32, ws_ref.shape, 1)
        c = lax.broadcasted_iota(jnp.int32, ws_ref.shape, 2)
        w_scr[...] = jnp.where(r >= c, ws_ref[...], 0.0).astype(BF16)

    v = _gelu(v_ref[...])
    mu = jnp.mean(v, axis=-1, keepdims=True)
    vc = v - mu
    vn = vc * lax.rsqrt(jnp.mean(vc * vc, axis=-1, keepdims=True) + LN_EPS)
    vn = (vn * lng_ref[...] + lnb_ref[...]).astype(BF16)
    n_chunks = u_ref.shape[0] // GM_CHUNK
    for c in range(n_chunks):
        rows = slice(c * GM_CHUNK, (c + 1) * GM_CHUNK)
        for g in range(GM_GROUPS):
            cols = slice(g * GM_CH, (g + 1) * GM_CH)
            mixed = _dot(w_scr[g], vn[rows, cols]) + bias_ref[:, cols]
            o_ref[rows, cols] = (_gelu(u_ref[rows, cols]) * mixed).astype(BF16)


def _gmlp(z, ln_g, ln_b, w_s, b_s, tg=512):
    T = z.shape[0]
    W = GM_GROUPS * GM_CH
    u_col = 4 * HG_HEADS * HG_D // W
    bias = jnp.repeat(b_s.T, GM_CH, axis=1)
    return pl.pallas_call(
        _gmlp_body,
        grid=(T // tg,),
        in_specs=[pl.BlockSpec((tg, W), lambda i: (i, u_col)),
                  pl.BlockSpec((tg, W), lambda i: (i, u_col + 1)),
                  pl.BlockSpec((1, W), lambda i: (0, 0)),
                  pl.BlockSpec((1, W), lambda i: (0, 0)),
                  pl.BlockSpec(w_s.shape, lambda i: (0, 0, 0)),
                  pl.BlockSpec(bias.shape, lambda i: (0, 0))],
        out_specs=pl.BlockSpec((tg, W), lambda i: (i, 0)),
        out_shape=jax.ShapeDtypeStruct((T, W), BF16),
        scratch_shapes=[pltpu.VMEM(w_s.shape, BF16)],
        compiler_params=_cparams(1),
        name="gmlp",
    )(z, z, ln_g.reshape(1, W), ln_b.reshape(1, W), w_s, bias)


def _outproj_body(o_ref, sg_ref, x_ref, wa_ref, wb_ref, g_ref, wr_ref, br_ref,
                  x1_ref, h_ref, ri_ref, rg_ref, cnt_ref, tri_scr, carry_scr):
    tm = x_ref.shape[0]

    @pl.when(pl.program_id(0) == 0)
    def _():
        r = lax.broadcasted_iota(jnp.int32, (tm, tm), 0)
        c = lax.broadcasted_iota(jnp.int32, (tm, tm), 1)
        tri_scr[...] = (c < r).astype(BF16)
        carry_scr[...] = jnp.zeros_like(carry_scr)

    x1 = x_ref[...] + _dot(o_ref[...], wa_ref[...]) + _dot(sg_ref[...], wb_ref[...])
    x1_ref[...] = x1
    h = _rms(x1, g_ref[...], RMS_EPS)
    half = h.shape[1] // 2
    h_ref[...] = _pack_bf16_pair(h[:, :half], h[:, half:])

    logits = jnp.dot(h, wr_ref[...], preferred_element_type=F32,
                     precision=lax.Precision.HIGHEST) + br_ref[...]
    lane = lax.broadcasted_iota(jnp.int32, (tm, LANES), 1)
    neg = jnp.float32(-jnp.inf)
    work = jnp.where(lane < N_EXPERTS, logits, neg)
    vals, idxs = [], []
    for _ in range(TOP_K):
        m = jnp.max(work, axis=-1, keepdims=True)
        i = jnp.min(jnp.where(work == m, lane, LANES), axis=-1, keepdims=True)
        vals.append(m)
        idxs.append(i)
        work = jnp.where(lane == i, neg, work)
    es = [jnp.exp(m - vals[0]) for m in vals]
    denom = es[0] + es[1] + es[2] + es[3]

    onehot = jnp.zeros((tm, LANES), F32)
    for i in idxs:
        onehot = onehot + (lane == i).astype(F32)
    before = _dot(tri_scr[...], onehot.astype(BF16)) + carry_scr[0:1, :]
    carry = carry_scr[0:1, :] + jnp.sum(onehot, axis=0, keepdims=True)
    carry_scr[...] = jnp.broadcast_to(carry, carry_scr.shape)
    cnt_ref[...] = jnp.broadcast_to(carry, cnt_ref.shape).astype(jnp.int32)

    ri = jnp.zeros((tm, LANES), jnp.int32)
    rg = jnp.zeros((tm, LANES), F32)
    for kk in range(TOP_K):
        rank = jnp.sum(jnp.where(lane == idxs[kk], before, 0.0), axis=-1, keepdims=True)
        ri = jnp.where(lane == kk, idxs[kk], ri)
        ri = jnp.where(lane == TOP_K + kk, rank.astype(jnp.int32), ri)
        rg = jnp.where(lane == kk, es[kk] / denom, rg)
    ri_ref[...] = ri
    rg_ref[...] = rg


def _outproj(o, sg, x2d, w_out, g, w_router, b_router, tm=512):
    T, D = x2d.shape
    Wh = o.shape[1]
    wa = w_out[:Wh].astype(BF16)
    wb = w_out[Wh:].astype(BF16)
    wr = jnp.zeros((D, LANES), F32).at[:, :N_EXPERTS].set(w_router)
    br = jnp.zeros((1, LANES), F32).at[0, :N_EXPERTS].set(b_router)
    row = lambda i: (i, 0)
    fixed = lambda i: (0, 0)
    return pl.pallas_call(
        _outproj_body,
        grid=(T // tm,),
        in_specs=[pl.BlockSpec((tm, Wh), row), pl.BlockSpec((tm, sg.shape[1]), row),
                  pl.BlockSpec((tm, D), row),
                  pl.BlockSpec(wa.shape, fixed), pl.BlockSpec(wb.shape, fixed),
                  pl.BlockSpec((1, D), fixed), pl.BlockSpec(wr.shape, fixed),
                  pl.BlockSpec(br.shape, fixed)],
        out_specs=[pl.BlockSpec((tm, D), row), pl.BlockSpec((tm, D // 2), row),
                   pl.BlockSpec((tm, LANES), row), pl.BlockSpec((tm, LANES), row),
                   pl.BlockSpec((8, LANES), fixed)],
        out_shape=[jax.ShapeDtypeStruct((T, D), F32), jax.ShapeDtypeStruct((T, D // 2), jnp.uint32),
                   jax.ShapeDtypeStruct((T, LANES), jnp.int32),
                   jax.ShapeDtypeStruct((T, LANES), F32),
                   jax.ShapeDtypeStruct((8, LANES), jnp.int32)],
        scratch_shapes=[pltpu.VMEM((tm, tm), BF16), pltpu.VMEM((8, LANES), F32)],
        compiler_params=_cparams(1),
        name="outproj_router",
    )(o, sg, x2d, wa, wb, g.reshape(1, D), wr, br)


def _dispatch_body(dest_ref, ends_ref, h_ref, xs_ref, zero_scr, sem, zsem):
    tm = h_ref.shape[0]
    base = pl.program_id(0) * (tm * TOP_K)

    @pl.when(pl.program_id(0) == 0)
    def _():
        zero_scr[...] = jnp.zeros_like(zero_scr)

        def last_block(e, op):
            end = ends_ref[e]
            prev = ends_ref[e - 1] if e else 0

            @pl.when(end > prev)
            def _():
                start = pl.multiple_of(end - EXPERT_BLOCK, EXPERT_BLOCK)
                op(pltpu.make_async_copy(zero_scr, xs_ref.at[pl.ds(start, EXPERT_BLOCK)], zsem))

        for e in range(N_EXPERTS):
            last_block(e, lambda cp: cp.start())
        for e in range(N_EXPERTS):
            last_block(e, lambda cp: cp.wait())

        def tail_copy(b):
            start = pl.multiple_of(b * EXPERT_BLOCK, EXPERT_BLOCK)
            return pltpu.make_async_copy(zero_scr, xs_ref.at[pl.ds(start, EXPERT_BLOCK)], zsem)

        first = ends_ref[N_EXPERTS - 1] // EXPERT_BLOCK
        n_blocks = xs_ref.shape[0] // EXPERT_BLOCK
        lax.fori_loop(first, n_blocks, lambda b, c: (tail_copy(b).start(), c)[1], 0)
        lax.fori_loop(first, n_blocks, lambda b, c: (tail_copy(b).wait(), c)[1], 0)

    def row_copy(r, d):
        return pltpu.make_async_copy(h_ref.at[pl.ds(r, 1)], xs_ref.at[pl.ds(d, 1)], sem)

    def issue(r, carry):
        for kk in range(TOP_K):
            row_copy(r, dest_ref[base + r * TOP_K + kk]).start(priority=kk % 2)
        return carry

    lax.fori_loop(0, tm, issue, 0, unroll=ROW_DMA_UNROLL)

    def drain(r, carry):
        for kk in range(TOP_K):
            row_copy(0, 0).wait()
        return carry

    lax.fori_loop(0, tm, drain, 0, unroll=ROW_DMA_UNROLL)


def _dispatch(dest_flat, pad_ends, h, n_rows, tm=256):
    T, W = h.shape
    return pl.pallas_call(
        _dispatch_body,
        grid_spec=pltpu.PrefetchScalarGridSpec(
            num_scalar_prefetch=2,
            grid=(T // tm,),
            in_specs=[pl.BlockSpec((tm, W), lambda i, dest, ends: (i, 0))],
            out_specs=pl.BlockSpec(memory_space=pl.ANY),
            scratch_shapes=[pltpu.VMEM((EXPERT_BLOCK, W), h.dtype),
                            pltpu.SemaphoreType.DMA(()), pltpu.SemaphoreType.DMA(())]),
        out_shape=jax.ShapeDtypeStruct((n_rows, W), h.dtype),
        compiler_params=_cparams(1),
        name="dispatch",
    )(dest_flat, pad_ends, h)


def _swiglu_deinterleaved(gu, fc):
    even = lax.broadcasted_iota(jnp.int32, (gu.shape[0], LANES), 1) % 2 == 0
    acts = []
    for c in range(fc // LANES):
        a = gu[:, c * LANES:(c + 1) * LANES]
        b = gu[:, fc + c * LANES:fc + (c + 1) * LANES]
        gate = jnp.where(even, a, pltpu.roll(b, 1, axis=1))
        up = jnp.where(even, pltpu.roll(a, LANES - 1, axis=1), b)
        gate = jnp.minimum(gate, SWIGLU_LIMIT)
        up = jnp.clip(up, -SWIGLU_LIMIT, SWIGLU_LIMIT)
        acts.append(((up + 1.0) * gate * jax.nn.sigmoid(SWIGLU_ALPHA * gate)).astype(BF16))
    return jnp.concatenate(acts, axis=1)


def _interleave_rows_bf16(a, b):
    return pltpu.bitcast(_pack_bf16_pair(a, b), BF16)


def _expert_body(ie_ref, ir_ref, in_ref, nu_ref,
                 xs_ref, wgu_ref, wdn_ref, bgu_ref, bdn_ref, ys_ref,
                 xres, act, wgu_bf, wdn_bf, ystage, sem_x, sem_y, *, n_gu, n_dn):
    del ie_ref
    i = pl.program_id(0)
    j = pl.program_id(1)
    nrb = in_ref[i]
    row0 = ir_ref[i]
    fc = act.shape[2]
    nc = ystage.shape[2]
    blk = EXPERT_BLOCK

    def rows(rb, first=0):
        return pl.ds(pl.multiple_of(first + rb * blk, blk), blk)

    @pl.when((i == 0) & (j == 0))
    def _():
        ystage[0] = jnp.zeros(ystage.shape[1:], ystage.dtype)
        n_blocks = ys_ref.shape[0] // blk

        def tail_copy(b, cc):
            return pltpu.make_async_copy(
                ystage.at[0, pl.ds(0, blk)], ys_ref.at[rows(b), pl.ds(cc * nc, nc)], sem_y.at[0])

        def tail_start(b, c):
            for cc in range(n_dn):
                tail_copy(b, cc).start()
            return c

        def tail_wait(b, c):
            for cc in range(n_dn):
                tail_copy(b, cc).wait()
            return c

        lax.fori_loop(nu_ref[0], n_blocks, tail_start, 0)
        lax.fori_loop(nu_ref[0], n_blocks, tail_wait, 0)

    def x_copy(rb):
        return pltpu.make_async_copy(xs_ref.at[rows(rb, row0)], xres.at[rows(rb)], sem_x)

    def for_blocks(fn):
        lax.fori_loop(0, nrb, lambda rb, c: (fn(rb), c)[1], 0)

    n_pairs = nrb // 2
    odd = nrb % 2

    def unit_rows(u, size, first=0):
        return pl.ds(pl.multiple_of(first + u * (2 * blk), blk), size)

    def for_units(fn):
        lax.fori_loop(0, n_pairs, lambda u, c: (fn(u, 2 * blk), c)[1], 0)

        @pl.when(odd == 1)
        def _():
            fn(n_pairs, blk)

    @pl.when((nrb > 0) & (j < n_gu))
    def _():
        @pl.when(j == 0)
        def _():
            for_blocks(lambda rb: x_copy(rb).start())

        wgu_bf[...] = wgu_ref[...].astype(BF16)

        @pl.when(j == 0)
        def _():
            for_blocks(lambda rb: x_copy(rb).wait())

        def unit(u, size):
            x = jnp.concatenate(_unpack_bf16_pair(xres[unit_rows(u, size), :]), axis=1)
            gu = _dot(x, wgu_bf[...]) + bgu_ref[...]
            act[j, unit_rows(u, size), :] = _swiglu_deinterleaved(gu, fc)

        for_units(unit)

    @pl.when((nrb > 0) & (j >= n_gu))
    def _():
        jd = j - n_gu
        n_slots = ystage.shape[0]
        for jf in range(n_gu):
            for c in range(fc // LANES):
                a0 = jf * fc + c * (LANES // 2)
                b0 = a0 + fc // 2
                wdn_bf[jf * fc + c * LANES:jf * fc + (c + 1) * LANES, :] = _interleave_rows_bf16(
                    wdn_ref[a0:a0 + LANES // 2, :], wdn_ref[b0:b0 + LANES // 2, :])

        def y_copy(u, size):
            slot = u % n_slots
            return pltpu.make_async_copy(
                ystage.at[slot, pl.ds(0, size)],
                ys_ref.at[unit_rows(u, size, row0), pl.ds(pl.multiple_of(jd * nc, nc), nc)],
                sem_y.at[slot])

        def unit(u, size):
            @pl.when(u >= n_slots)
            def _():
                y_copy(u - n_slots, 2 * blk).wait()

            a = jnp.concatenate([act[jf, unit_rows(u, size), :] for jf in range(n_gu)], axis=1)
            y = _dot(a, wdn_bf[...]) + bdn_ref[...]
            ystage[u % n_slots, 0:size, :] = _pack_bf16_pair(y[:, :nc], y[:, nc:])
            y_copy(u, size).start()

        for_units(unit)
        n_units = n_pairs + odd
        for back in range(1, n_slots + 1):
            @pl.when((n_units >= back) & ((odd == 0) | (back > 1)))
            def _():
                y_copy(n_units - back, 2 * blk).wait()

        @pl.when(odd == 1)
        def _():
            y_copy(n_units - 1, blk).wait()


def _experts(items, xs, w_gate_up, w_down, b_gate_up, b_down, fc, nc):
    item_e, item_row0, item_nrb, n_used = items
    n_rows = xs.shape[0]
    E, D, F2 = w_gate_up.shape
    F = F2 // 2
    n_gu = F // fc
    n_dn = D // nc
    n_items = item_e.shape[0]

    def gu_chunk(i, j, ie, ir, inr, nu):
        return jnp.where(inr[i] > 0, jnp.minimum(j, n_gu - 1), n_gu - 1)

    def dn_chunk(i, j, ie, ir, inr, nu):
        return jnp.where(inr[i] > 0, jnp.maximum(j - n_gu, 0), n_dn - 1)

    return pl.pallas_call(
        functools.partial(_expert_body, n_gu=n_gu, n_dn=n_dn),
        grid_spec=pltpu.PrefetchScalarGridSpec(
            num_scalar_prefetch=4,
            grid=(n_items, n_gu + n_dn),
            in_specs=[
                pl.BlockSpec(memory_space=pl.ANY),
                pl.BlockSpec((None, D, 2 * fc), lambda i, j, ie, *a: (ie[i], 0, gu_chunk(i, j, ie, *a))),
                pl.BlockSpec((None, F, nc), lambda i, j, ie, *a: (ie[i], 0, dn_chunk(i, j, ie, *a))),
                pl.BlockSpec((None, 1, 2 * fc), lambda i, j, ie, *a: (ie[i], 0, gu_chunk(i, j, ie, *a))),
                pl.BlockSpec((None, 1, nc), lambda i, j, ie, *a: (ie[i], 0, dn_chunk(i, j, ie, *a))),
            ],
            out_specs=pl.BlockSpec(memory_space=pl.ANY),
            scratch_shapes=[
                pltpu.VMEM((EXPERT_ROWS, D // 2), jnp.uint32),
                pltpu.VMEM((n_gu, EXPERT_ROWS, fc), BF16),
                pltpu.VMEM((D, 2 * fc), BF16), pltpu.VMEM((F, nc), BF16),
                pltpu.VMEM((EXPERT_YSLOTS, 2 * EXPERT_BLOCK, nc // 2), jnp.uint32),
                pltpu.SemaphoreType.DMA(()), pltpu.SemaphoreType.DMA((EXPERT_YSLOTS,))]),
        out_shape=jax.ShapeDtypeStruct((n_rows, D // 2), jnp.uint32),
        compiler_params=_cparams(2),
        name="experts",
    )(item_e, item_row0, item_nrb, n_used, xs, w_gate_up, w_down, b_gate_up[:, None, :],
      b_down[:, None, :])


def _combine_body(dest_ref, x1_ref, rg_ref, p_ref, ys_ref, gp_ref, wpg_ref, wpp_ref, gf_ref,
                  out_ref, buf, sem):
    tm = x1_ref.shape[0]
    i = pl.program_id(0)
    slot = i % 2

    def row_copy(s, r, kk, d):
        return pltpu.make_async_copy(ys_ref.at[pl.ds(d, 1)], buf.at[s, kk, pl.ds(r, 1)], sem.at[s])

    def gather(step, s):
        base = step * (tm * TOP_K)

        def issue(r, carry):
            for kk in range(TOP_K):
                row_copy(s, r, kk, dest_ref[base + r * TOP_K + kk]).start(priority=kk % 2)
            return carry

        lax.fori_loop(0, tm, issue, 0, unroll=ROW_DMA_UNROLL)

    @pl.when(i == 0)
    def _():
        gather(0, 0)

    @pl.when(i + 1 < pl.num_programs(0))
    def _():
        gather(i + 1, 1 - slot)

    def drain(r, carry):
        for kk in range(TOP_K):
            row_copy(slot, 0, kk, 0).wait()
        return carry

    lax.fori_loop(0, tm, drain, 0, unroll=ROW_DMA_UNROLL)

    def unpacked(w):
        pc = EXPERT_NC // 2
        parts = []
        for c in range(w.shape[1] // pc):
            chunk = w[:, c * pc:(c + 1) * pc]
            parts.append(pltpu.bitcast(chunk << 16, F32))
            parts.append(pltpu.bitcast(chunk & jnp.uint32(0xFFFF0000), F32))
        return jnp.concatenate(parts, axis=1)

    gates = rg_ref[...]
    x2 = x1_ref[...]
    for kk in range(TOP_K):
        x2 = x2 + gates[:, kk:kk + 1] * unpacked(buf[slot, kk])
    hp = _rms(x2, gp_ref[...], RMS_EPS).astype(BF16)
    gate = jax.nn.sigmoid(_dot(hp, wpg_ref[...]))
    proj = _dot(p_ref[...].astype(BF16), wpp_ref[...])
    x3 = x2 + gate * proj
    out_ref[...] = _rms(x3, gf_ref[...], RMS_EPS)


def _combine(dest_flat, x1, rg, p2d, ys, g_ple, w_gate, w_proj, g_final, tm=256):
    T, D = x1.shape
    P = p2d.shape[1]
    row = lambda i, dest: (i, 0)
    fixed = lambda i, dest: (0, 0)
    return pl.pallas_call(
        _combine_body,
        grid_spec=pltpu.PrefetchScalarGridSpec(
            num_scalar_prefetch=1,
            grid=(T // tm,),
            in_specs=[pl.BlockSpec((tm, D), row), pl.BlockSpec((tm, LANES), row),
                      pl.BlockSpec((tm, P), row), pl.BlockSpec(memory_space=pl.ANY),
                      pl.BlockSpec((1, D), fixed), pl.BlockSpec((D, D), fixed),
                      pl.BlockSpec((P, D), fixed), pl.BlockSpec((1, D), fixed)],
            out_specs=pl.BlockSpec((tm, D), row),
            scratch_shapes=[pltpu.VMEM((2, TOP_K, tm, D // 2), jnp.uint32),
                            pltpu.SemaphoreType.DMA((2,))]),
        out_shape=jax.ShapeDtypeStruct((T, D), F32),
        compiler_params=_cparams(1),
        name="combine_ple",
    )(dest_flat, x1, rg, p2d, ys, g_ple.reshape(1, D), w_gate.astype(BF16),
      w_proj.astype(BF16), g_final.reshape(1, D))


def _routing_tables(ri, counts_block, n_items):
    counts = counts_block[0, :N_EXPERTS]
    padded = (counts + EXPERT_BLOCK - 1) // EXPERT_BLOCK * EXPERT_BLOCK
    pad_ends = jnp.cumsum(padded)
    pad_starts = pad_ends - padded
    idx = ri[:, :TOP_K]
    rank = ri[:, TOP_K:2 * TOP_K]
    dest = (pad_starts[idx] + rank).reshape(-1).astype(jnp.int32)

    per_expert = (padded + EXPERT_ROWS - 1) // EXPERT_ROWS
    item_ends = jnp.cumsum(per_expert)
    total = item_ends[-1]
    ids = jnp.arange(n_items, dtype=jnp.int32)
    e_of = jnp.sum(jnp.minimum(ids, total - 1)[:, None] >= item_ends[None, :], axis=1)
    e_of = jnp.clip(e_of, 0, N_EXPERTS - 1).astype(jnp.int32)
    local = ids - (item_ends - per_expert)[e_of]
    active = ids < total
    row0 = jnp.where(active, pad_starts[e_of] + local * EXPERT_ROWS, 0)
    left = jnp.clip(padded[e_of] - local * EXPERT_ROWS, 0, EXPERT_ROWS)
    nrb = jnp.where(active, left // EXPERT_BLOCK, 0)
    n_used = pad_ends[-1:] // EXPERT_BLOCK
    i32 = lambda a: a.astype(jnp.int32)
    return dest, i32(pad_ends), (e_of, i32(row0), i32(nrb), i32(n_used))


def _layer(x2d, p2d, B, S, norm_mix, w_in, lb, hgrn_out_norm, gmlp_ln_g, gmlp_ln_b, w_spatial,
           b_spatial, w_out, norm_ffn, w_router, b_router, w_gate_up, b_gate_up, w_down, b_down,
           norm_ple, w_ple_gate, w_ple_proj, g_out):
    T = x2d.shape[0]
    z = _inproj(x2d, norm_mix, w_in.astype(BF16))
    o = _hgrn(z, lb, hgrn_out_norm, B, S)
    sg = _gmlp(z, gmlp_ln_g, gmlp_ln_b, w_spatial, b_spatial)
    x1, h2, ri, rg, counts = _outproj(o, sg, x2d, w_out, norm_ffn, w_router, b_router)

    n_rows = T * TOP_K + N_EXPERTS * EXPERT_BLOCK
    n_items = N_EXPERTS + T * TOP_K // EXPERT_ROWS
    dest, pad_ends, items = _routing_tables(ri, counts, n_items)
    xs = _dispatch(dest, pad_ends, h2, n_rows)
    ys = _experts(items, xs, w_gate_up, w_down, b_gate_up, b_down, EXPERT_FC, EXPERT_NC)
    return _combine(dest, x1, rg, p2d, ys, norm_ple, w_ple_gate, w_ple_proj, g_out)


def kernel(x, p, norm_mix, w_in, lb_logits, hgrn_out_norm, gmlp_ln_g, gmlp_ln_b, w_spatial, b_spatial, w_out, norm_ffn, w_router, b_router, w_gate_up, b_gate_up, w_down, b_down, norm_ple, w_ple_gate, w_ple_proj, norm_final):
    B, S, D = x.shape
    depth = p.shape[0]
    assert depth == 1, "the final norm is fused into the last layer's combine kernel"
    lower_bounds = jnp.cumsum(jax.nn.softmax(lb_logits.astype(F32), axis=0), axis=0)
    out = _layer(x.reshape(B * S, D), p[0].reshape(B * S, -1), B, S, norm_mix[0], w_in[0],
                 lower_bounds[0], hgrn_out_norm[0], gmlp_ln_g[0], gmlp_ln_b[0], w_spatial[0],
                 b_spatial[0], w_out[0], norm_ffn[0], w_router[0], b_router[0], w_gate_up[0],
                 b_gate_up[0], w_down[0], b_down[0], norm_ple[0], w_ple_gate[0], w_ple_proj[0],
                 norm_final)
    return out.reshape(B, S, D)
```
